```python
import math
import jax, jax.numpy as jnp
from jax import lax
import numpy as np

D_MODEL = 4096
BATCH = 4
SEQ = 2048
DEPTH = 2

N_HEADS = 32
HEAD_DIM = D_MODEL // N_HEADS
MOBA_BLOCK = 256
MOBA_TOP_K = 3
Q_CHUNK = 128
CONV_WIDTH = 3
D_FF = 7 * D_MODEL // 2
N_EXPERTS = 8
EXPERT_TOP_K = 2
D_FF_EXPERT = 11 * D_MODEL // 8
RMS_EPS = 1e-6
NEG = -1e30

kernel_name = "hybrid_moba_shortconv_moe"


def rms_norm(x, g):
    xf = x.astype(jnp.float32)
    y = xf * lax.rsqrt(jnp.mean(xf * xf, axis=-1, keepdims=True) + RMS_EPS)
    return (y * g.astype(jnp.float32)).astype(x.dtype)


def alibi_slopes(n_heads):
    return jnp.asarray(np.power(2.0, -8.0 * np.arange(1, n_heads + 1) / n_heads).astype(np.float32))


def moba_attention(q, k, v):
    b, s, h, hd = q.shape
    nb = -(-s // MOBA_BLOCK)
    sp = nb * MOBA_BLOCK
    nc = -(-s // Q_CHUNK)
    sq = nc * Q_CHUNK
    tk = min(MOBA_TOP_K, nb - 1)
    scale = 1.0 / math.sqrt(hd)
    qh = jnp.pad(q, ((0, 0), (0, sq - s), (0, 0), (0, 0))).transpose(2, 0, 1, 3)
    kh = jnp.pad(k, ((0, 0), (0, sp - s), (0, 0), (0, 0))).transpose(2, 0, 1, 3)
    vh = jnp.pad(v, ((0, 0), (0, sp - s), (0, 0), (0, 0))).transpose(2, 0, 1, 3)
    slopes = alibi_slopes(h)

    def per_head(args):
        q_h, k_h, v_h, m_h = args
        kb = k_h.reshape(b, nb, MOBA_BLOCK, hd)
        vb = v_h.reshape(b, nb, MOBA_BLOCK, hd)
        k_mean = jnp.mean(kb.astype(jnp.float32), axis=2)

        def per_chunk(c):
            start = c * Q_CHUNK
            blk = start // MOBA_BLOCK
            qc = lax.dynamic_slice_in_dim(q_h, start, Q_CHUNK, axis=1)
            t = start + jnp.arange(Q_CHUNK)
            k_own = lax.dynamic_slice_in_dim(k_h, blk * MOBA_BLOCK, MOBA_BLOCK, axis=1)
            v_own = lax.dynamic_slice_in_dim(v_h, blk * MOBA_BLOCK, MOBA_BLOCK, axis=1)
            dist_own = t[:, None] - (blk * MOBA_BLOCK + jnp.arange(MOBA_BLOCK))[None, :]
            logit_own = (jnp.einsum('bqd,bkd->bqk', qc, k_own, preferred_element_type=jnp.float32) * scale
                         - m_h * dist_own.astype(jnp.float32))
            logit_own = jnp.where(dist_own >= 0, logit_own, NEG)
            if tk == 0:
                p = jax.nn.softmax(logit_own, axis=-1).astype(v_h.dtype)
                return jnp.einsum('bqk,bkd->bqd', p, v_own)
            gate = jnp.einsum('bqd,bnd->bqn', qc.astype(jnp.float32), k_mean)
            gate = jnp.where(jnp.arange(nb) < blk, gate, -jnp.inf)
            _, idx = lax.top_k(gate, tk)
            valid = idx < blk
            k_sel = jax.vmap(lambda kk, ii: kk[ii])(kb, idx)
            v_sel = jax.vmap(lambda vv, ii: vv[ii])(vb, idx)
            pos_sel = idx[..., None] * MOBA_BLOCK + jnp.arange(MOBA_BLOCK)
            dist_sel = t[None, :, None, None] - pos_sel
            logit_sel = (jnp.einsum('bqd,bqjkd->bqjk', qc, k_sel, preferred_element_type=jnp.float32) * scale
                         - m_h * dist_sel.astype(jnp.float32))
            logit_sel = jnp.where(valid[..., None], logit_sel, NEG)
            logits = jnp.concatenate([logit_own, logit_sel.reshape(b, Q_CHUNK, tk * MOBA_BLOCK)], axis=-1)
            p = jax.nn.softmax(logits, axis=-1).astype(v_h.dtype)
            p_own = p[..., :MOBA_BLOCK]
            p_sel = p[..., MOBA_BLOCK:].reshape(b, Q_CHUNK, tk, MOBA_BLOCK)
            return (jnp.einsum('bqk,bkd->bqd', p_own, v_own)
                    + jnp.einsum('bqjk,bqjkd->bqd', p_sel, v_sel))

        outs = lax.map(per_chunk, jnp.arange(nc))
        return outs.transpose(1, 0, 2, 3).reshape(b, sq, hd)

    o = lax.map(per_head, (qh, kh, vh, slopes))
    return o.transpose(1, 2, 0, 3)[:, :s]


def moba_mixer(x, w_qkv, q_gain, k_gain, w_out):
    b, s, d = x.shape
    q, k, v = jnp.split(x @ w_qkv, 3, axis=-1)
    q = rms_norm(q.reshape(b, s, N_HEADS, HEAD_DIM), q_gain)
    k = rms_norm(k.reshape(b, s, N_HEADS, HEAD_DIM), k_gain)
    v = v.reshape(b, s, N_HEADS, HEAD_DIM)
    o = moba_attention(q, k, v)
    return o.reshape(b, s, d) @ w_out


def short_conv_mixer(x, w_in, conv_w, w_out):
    bg, cg, hx = jnp.split(x @ w_in, 3, axis=-1)
    u = cg * hx
    y = lax.conv_general_dilated(u, conv_w[:, None, :], window_strides=(1,),
                                 padding=[(CONV_WIDTH - 1, 0)],
                                 dimension_numbers=('NWC', 'WIO', 'NWC'),
                                 feature_group_count=u.shape[-1])
    return (bg * y) @ w_out


def swiglu(x, w_gate_up, w_down):
    g, u = jnp.split(x @ w_gate_up, 2, axis=-1)
    return (jax.nn.silu(g) * u) @ w_down


def moe_swiglu(x, w_router, w_gate_up, w_down):
    b, s, d = x.shape
    xt = x.reshape(b * s, d)
    logits = (xt @ w_router).astype(jnp.float32)
    top_val, top_idx = lax.top_k(logits, EXPERT_TOP_K)
    top_w = jax.nn.softmax(top_val, axis=-1)
    combine = jnp.sum(jax.nn.one_hot(top_idx, N_EXPERTS, dtype=jnp.float32) * top_w[..., None], axis=1)
    combine = combine.astype(x.dtype)
    out = combine[:, 0:1] * swiglu(xt, w_gate_up[0], w_down[0])
    for e in range(1, N_EXPERTS):
        out = out + combine[:, e:e + 1] * swiglu(xt, w_gate_up[e], w_down[e])
    return out.reshape(b, s, d)


def setup_inputs(seed: int = 0) -> dict:
    key = jax.random.key(seed)
    ks = jax.random.split(key, 20)
    n_even = (DEPTH + 1) // 2
    n_odd = DEPTH // 2
    f32 = jnp.float32

    def w(k, shape, fan_in):
        return jax.random.normal(k, shape, f32) * (fan_in ** -0.5)

    def gain(k, shape):
        return 1.0 + 0.02 * jax.random.normal(k, shape, f32)

    return {
        "x": jax.random.normal(ks[0], (BATCH, SEQ, D_MODEL), f32),
        "attn_norm": gain(ks[1], (n_even, D_MODEL)),
        "w_qkv": w(ks[2], (n_even, D_MODEL, 3 * D_MODEL), D_MODEL),
        "q_norm": gain(ks[3], (n_even, HEAD_DIM)),
        "k_norm": gain(ks[4], (n_even, HEAD_DIM)),
        "w_attn_out": w(ks[5], (n_even, D_MODEL, D_MODEL), D_MODEL),
        "ffn_norm": gain(ks[6], (n_even, D_MODEL)),
        "w_ffn_gate_up": w(ks[7], (n_even, D_MODEL, 2 * D_FF), D_MODEL),
        "w_ffn_down": w(ks[8], (n_even, D_FF, D_MODEL), D_FF),
        "conv_norm": gain(ks[9], (n_odd, D_MODEL)),
        "w_conv_in": w(ks[10], (n_odd, D_MODEL, 3 * D_MODEL), D_MODEL),
        "conv_w": w(ks[11], (n_odd, CONV_WIDTH, D_MODEL), CONV_WIDTH),
        "w_conv_out": w(ks[12], (n_odd, D_MODEL, D_MODEL), D_MODEL),
        "moe_norm": gain(ks[13], (n_odd, D_MODEL)),
        "w_router": w(ks[14], (n_odd, D_MODEL, N_EXPERTS), D_MODEL),
        "w_expert_gate_up": w(ks[15], (n_odd, N_EXPERTS, D_MODEL, 2 * D_FF_EXPERT), D_MODEL),
        "w_expert_down": w(ks[16], (n_odd, N_EXPERTS, D_FF_EXPERT, D_MODEL), D_FF_EXPERT),
    }


def reference(x, attn_norm, w_qkv, q_norm, k_norm, w_attn_out, ffn_norm, w_ffn_gate_up, w_ffn_down,
              conv_norm, w_conv_in, conv_w, w_conv_out, moe_norm, w_router, w_expert_gate_up, w_expert_down):
    h = x
    for i in range(DEPTH):
        j = i // 2
        if i % 2 == 0:
            h = h + moba_mixer(rms_norm(h, attn_norm[j]), w_qkv[j], q_norm[j], k_norm[j], w_attn_out[j])
            h = h + swiglu(rms_norm(h, ffn_norm[j]), w_ffn_gate_up[j], w_ffn_down[j])
        else:
            h = h + short_conv_mixer(rms_norm(h, conv_norm[j]), w_conv_in[j], conv_w[j], w_conv_out[j])
            h = h + moe_swiglu(rms_norm(h, moe_norm[j]), w_router[j], w_expert_gate_up[j], w_expert_down[j])
    return h
```

```python
import functools
import math

import numpy as np
import jax
import jax.numpy as jnp
from jax import lax
from jax.experimental import pallas as pl
from jax.experimental.pallas import tpu as pltpu

F32 = jnp.float32
BF16 = jnp.bfloat16

MOBA_BLOCK = 256
MOBA_TOP_K = 3
EXPERT_TOP_K = 2
CONV_WIDTH = 3
RMS_EPS = 1e-6
NEG = -1e30

LANES = 128
BF16_SUBLANES = 16
V7X_VMEM_BYTES = 64 * 1024 * 1024
VMEM_LIMIT_BYTES = V7X_VMEM_BYTES - 8 * 1024 * 1024

ROW_TILE = 1024
COL_TILE = 512
MOE_ROW_TILE = 512
NORM_ROW_TILE = 256
GATHER_ROW_TILE = 256
CAST_ROWS = 256


def _tile(dim, pref, mult=LANES):
    if dim <= pref:
        return dim
    t = pref - pref % mult
    while t >= mult:
        if dim % t == 0:
            return t
        t -= mult
    return dim


def _cparams(n_axes):
    return pltpu.CompilerParams(dimension_semantics=("arbitrary",) * n_axes,
                                vmem_limit_bytes=VMEM_LIMIT_BYTES)


def _cast_rows(w_ref, wb_ref):
    k = w_ref.shape[0]
    rows = _tile(k, CAST_ROWS, BF16_SUBLANES)

    def step(r, carry):
        sl = pl.ds(pl.multiple_of(r * rows, rows), rows)
        wb_ref[sl, :] = w_ref[sl, :].astype(wb_ref.dtype)
        return carry

    lax.fori_loop(0, k // rows, step, 0)


def _rmsnorm_body(x_ref, g_ref, o_ref):
    x = x_ref[...]
    inv = lax.rsqrt(jnp.mean(x * x, axis=-1, keepdims=True) + RMS_EPS)
    o_ref[...] = (x * inv * g_ref[...]).astype(o_ref.dtype)


def _rmsnorm(x, g):
    n, d = x.shape
    tm = _tile(n, NORM_ROW_TILE, BF16_SUBLANES)
    return pl.pallas_call(
        _rmsnorm_body,
        grid=(n // tm,),
        in_specs=[pl.BlockSpec((tm, d), lambda i: (i, 0)),
                  pl.BlockSpec((1, d), lambda i: (0, 0))],
        out_specs=pl.BlockSpec((tm, d), lambda i: (i, 0)),
        out_shape=jax.ShapeDtypeStruct((n, d), BF16),
        compiler_params=_cparams(1),
        name="rmsnorm",
    )(x, g.reshape(1, d))


def _mm_plain_body(x_ref, w_ref, o_ref, wb_ref):
    @pl.when(pl.program_id(1) == 0)
    def _():
        _cast_rows(w_ref, wb_ref)

    o_ref[...] = jnp.dot(x_ref[...], wb_ref[...], preferred_element_type=F32).astype(o_ref.dtype)


def _mm_residual_body(x_ref, w_ref, r_ref, o_ref, wb_ref):
    @pl.when(pl.program_id(1) == 0)
    def _():
        _cast_rows(w_ref, wb_ref)

    o_ref[...] = r_ref[...] + jnp.dot(x_ref[...], wb_ref[...], preferred_element_type=F32)


def _mm_qknorm_body(x_ref, w_ref, g_ref, o_ref, wb_ref, *, n_norm_tiles, head_dim):
    j = pl.program_id(0)

    @pl.when(pl.program_id(1) == 0)
    def _():
        _cast_rows(w_ref, wb_ref)

    acc = jnp.dot(x_ref[...], wb_ref[...], preferred_element_type=F32)

    @pl.when(j < n_norm_tiles)
    def _():
        for hh in range(acc.shape[1] // head_dim):
            sl = slice(hh * head_dim, (hh + 1) * head_dim)
            a = acc[:, sl]
            inv = lax.rsqrt(jnp.mean(a * a, axis=-1, keepdims=True) + RMS_EPS)
            o_ref[:, sl] = (a * inv * g_ref[:, sl]).astype(o_ref.dtype)

    @pl.when(j >= n_norm_tiles)
    def _():
        o_ref[...] = acc.astype(o_ref.dtype)


def _swiglu_act(g, u):
    return g * (1.0 / (1.0 + jnp.exp(-g))) * u


def _mm_swiglu_body(x_ref, wg_ref, wu_ref, o_ref, wgb_ref, wub_ref):
    @pl.when(pl.program_id(1) == 0)
    def _():
        _cast_rows(wg_ref, wgb_ref)
        _cast_rows(wu_ref, wub_ref)

    x = x_ref[...]
    g = jnp.dot(x, wgb_ref[...], preferred_element_type=F32)
    u = jnp.dot(x, wub_ref[...], preferred_element_type=F32)
    o_ref[...] = _swiglu_act(g, u).astype(o_ref.dtype)


def _mm(x, w, *, out_dtype=BF16, residual=None, qk_gain=None, n_norm_cols=0, head_dim=LANES):
    m, k = x.shape
    n = w.shape[1]
    tm = _tile(m, ROW_TILE, BF16_SUBLANES)
    tn = _tile(n, COL_TILE if residual is None else COL_TILE // 2)
    grid = (n // tn, m // tm)
    in_specs = [pl.BlockSpec((tm, k), lambda j, i: (i, 0)),
                pl.BlockSpec((k, tn), lambda j, i: (0, j))]
    args = [x, w]
    if residual is not None:
        body = _mm_residual_body
        in_specs.append(pl.BlockSpec((tm, tn), lambda j, i: (i, j)))
        args.append(residual)
        out_dtype = F32
        name = "mm_residual"
    elif qk_gain is not None:
        assert n_norm_cols % tn == 0 and tn % head_dim == 0
        body = functools.partial(_mm_qknorm_body, n_norm_tiles=n_norm_cols // tn, head_dim=head_dim)
        in_specs.append(pl.BlockSpec((1, tn), lambda j, i: (0, j)))
        args.append(qk_gain)
        name = "mm_qknorm"
    else:
        body = _mm_plain_body
        name = "mm_plain"
    return pl.pallas_call(
        body,
        grid=grid,
        in_specs=in_specs,
        out_specs=pl.BlockSpec((tm, tn), lambda j, i: (i, j)),
        out_shape=jax.ShapeDtypeStruct((m, n), out_dtype),
        scratch_shapes=[pltpu.VMEM((k, tn), BF16)],
        compiler_params=_cparams(2),
        name=name,
    )(*args)


def _mm_swiglu(x, w_gate_up):
    m, k = x.shape
    f = w_gate_up.shape[1] // 2
    tm = _tile(m, ROW_TILE, BF16_SUBLANES)
    tn = _tile(f, COL_TILE // 2)
    nj = f // tn
    return pl.pallas_call(
        _mm_swiglu_body,
        grid=(nj, m // tm),
        in_specs=[pl.BlockSpec((tm, k), lambda j, i: (i, 0)),
                  pl.BlockSpec((k, tn), lambda j, i: (0, j)),
                  pl.BlockSpec((k, tn), lambda j, i: (0, nj + j))],
        out_specs=pl.BlockSpec((tm, tn), lambda j, i: (i, j)),
        out_shape=jax.ShapeDtypeStruct((m, f), BF16),
        scratch_shapes=[pltpu.VMEM((k, tn), BF16), pltpu.VMEM((k, tn), BF16)],
        compiler_params=_cparams(2),
        name="mm_swiglu",
    )(x, w_gate_up, w_gate_up)


def _mm_ktiled_residual_body(x_ref, w_ref, r_ref, o_ref):
    kk = pl.program_id(2)
    part = jnp.dot(x_ref[...], w_ref[...].astype(BF16), preferred_element_type=F32)

    @pl.when(kk == 0)
    def _():
        o_ref[...] = r_ref[...] + part

    @pl.when(kk > 0)
    def _():
        o_ref[...] += part


def _mm_ktiled_residual(x, w, residual):
    m, k = x.shape
    n = w.shape[1]
    tm = _tile(m, ROW_TILE, BF16_SUBLANES)
    tn = _tile(n, 2 * COL_TILE)
    tk = _tile(k, 1024)
    return pl.pallas_call(
        _mm_ktiled_residual_body,
        grid=(m // tm, n // tn, k // tk),
        in_specs=[pl.BlockSpec((tm, tk), lambda i, j, kk: (i, kk)),
                  pl.BlockSpec((tk, tn), lambda i, j, kk: (kk, j)),
                  pl.BlockSpec((tm, tn), lambda i, j, kk: (i, j))],
        out_specs=pl.BlockSpec((tm, tn), lambda i, j, kk: (i, j)),
        out_shape=jax.ShapeDtypeStruct((m, n), F32),
        compiler_params=_cparams(3),
        name="mm_ktiled_residual",
    )(x, w, residual)


def _attn_body(slopes_ref, q_ref, k_ref, v_ref, o_ref, vt_ref, sel_ref, apast_ref, aown_ref,
               *, n_blocks, top_k):
    blk = MOBA_BLOCK
    s_len = n_blocks * blk
    slope = slopes_ref[pl.program_id(1)]
    nt_dims = (((1,), (1,)), ((), ()))

    km = jnp.concatenate(
        [jnp.mean(k_ref[n * blk:(n + 1) * blk, :].astype(F32), axis=0, keepdims=True)
         for n in range(n_blocks)], axis=0)
    km_hi = km.astype(BF16)
    km_lo = (km - km_hi.astype(F32)).astype(BF16)
    km2 = jnp.concatenate([km_hi, km_lo], axis=1)
    q = q_ref[...]
    q2 = jnp.concatenate([q, q], axis=1)
    gate = lax.dot_general(km2, q2, nt_dims, preferred_element_type=F32)

    n_iota = lax.broadcasted_iota(jnp.int32, (n_blocks, s_len), 0)
    q_iota = lax.broadcasted_iota(jnp.int32, (n_blocks, s_len), 1)
    valid = n_iota < lax.shift_right_logical(q_iota, int(math.log2(blk)))
    g = jnp.where(valid, gate, -jnp.inf)
    rank = jnp.zeros((n_blocks, s_len), jnp.int32)
    for mm in range(n_blocks):
        gm = g[mm:mm + 1, :]
        tie = jnp.where(n_iota > mm, 1, 0)
        rank = rank + jnp.where(gm > g, 1, jnp.where(gm == g, tie, 0))
    sel_ref[...] = jnp.where(valid, jnp.where(rank < top_k, 0.0, NEG), NEG)

    r = lax.broadcasted_iota(jnp.int32, (blk, blk), 1)
    c = lax.broadcasted_iota(jnp.int32, (blk, blk), 0)
    a_past = (-slope) * (r - c).astype(F32)
    apast_ref[...] = a_past
    aown_ref[...] = jnp.where(c <= r, a_past, NEG)

    vt_ref[...] = v_ref[...].astype(F32).T.astype(vt_ref.dtype)

    for i in range(n_blocks):
        sl = slice(i * blk, (i + 1) * blk)
        qi = q_ref[sl, :]
        s = lax.dot_general(k_ref[sl, :], qi, nt_dims, preferred_element_type=F32) + aown_ref[...]
        m = jnp.max(s, axis=0, keepdims=True)
        p = jnp.exp(s - m)
        l = jnp.sum(p, axis=0, keepdims=True)
        acc = jnp.dot(vt_ref[:, sl], p.astype(BF16), preferred_element_type=F32)
        for n in range(i):
            sn = slice(n * blk, (n + 1) * blk)
            cn = slope * float(-blk * (i - n))
            s = (lax.dot_general(k_ref[sn, :], qi, nt_dims, preferred_element_type=F32)
                 + apast_ref[...] + sel_ref[n:n + 1, sl])
            m_new = jnp.maximum(m, jnp.max(s, axis=0, keepdims=True) + cn)
            alpha = jnp.exp(m - m_new)
            p = jnp.exp(s - (m_new - cn))
            l = alpha * l + jnp.sum(p, axis=0, keepdims=True)
            acc = alpha * acc + jnp.dot(vt_ref[:, sn], p.astype(BF16), preferred_element_type=F32)
            m = m_new
        o_ref[sl, :] = (acc * (1.0 / l)).T.astype(o_ref.dtype)


def _moba_attention(qkv, batch, seq, n_heads, head_dim):
    assert seq % MOBA_BLOCK == 0 and head_dim == LANES
    n_blocks = seq // MOBA_BLOCK
    top_k = min(MOBA_TOP_K, n_blocks - 1)
    slopes = jnp.asarray(np.power(2.0, -8.0 * np.arange(1, n_heads + 1) / n_heads).astype(np.float32))
    blk_spec = lambda off: pl.BlockSpec((seq, head_dim), lambda b, h: (b, off + h))
    return pl.pallas_call(
        functools.partial(_attn_body, n_blocks=n_blocks, top_k=top_k),
        grid=(batch, n_heads),
        in_specs=[pl.BlockSpec(memory_space=pltpu.SMEM),
                  blk_spec(0), blk_spec(n_heads), blk_spec(2 * n_heads)],
        out_specs=pl.BlockSpec((seq, head_dim), lambda b, h: (b, h)),
        out_shape=jax.ShapeDtypeStruct((batch * seq, n_heads * head_dim), BF16),
        scratch_shapes=[pltpu.VMEM((head_dim, seq), BF16),
                        pltpu.VMEM((n_blocks, seq), F32),
                        pltpu.VMEM((MOBA_BLOCK, MOBA_BLOCK), F32),
                        pltpu.VMEM((MOBA_BLOCK, MOBA_BLOCK), F32)],
        compiler_params=_cparams(2),
        name="moba_attention",
    )(slopes, qkv, qkv, qkv)


def _conv_gate_body(b_ref, c_ref, h_ref, cp_ref, hp_ref, w_ref, o_ref, *, seq):
    tm = c_ref.shape[0]
    u = c_ref[...].astype(F32) * h_ref[...].astype(F32)
    keep = jnp.where((pl.program_id(0) * tm) % seq == 0, 0.0, 1.0)
    up = cp_ref[...].astype(F32) * hp_ref[...].astype(F32) * keep
    last = up.shape[0] - 1
    row = lax.broadcasted_iota(jnp.int32, u.shape, 0)
    u1 = jnp.where(row == 0, up[last:last + 1, :], pltpu.roll(u, 1, 0))
    u2 = jnp.where(row == 0, up[last - 1:last, :],
                   jnp.where(row == 1, up[last:last + 1, :], pltpu.roll(u, 2, 0)))
    y = w_ref[0:1, :] * u2 + w_ref[1:2, :] * u1 + w_ref[2:3, :] * u
    o_ref[...] = (b_ref[...].astype(F32) * y).astype(o_ref.dtype)


def _conv_gate(bch, conv_w, seq):
    n, d3 = bch.shape
    d = d3 // 3
    tm = _tile(seq, 512, BF16_SUBLANES)
    tc = _tile(d, 1024)
    nc = d // tc
    halo = BF16_SUBLANES
    hb = tm // halo
    prev = lambda off: (lambda i, j: (jnp.maximum(i * hb - 1, 0), off + j))
    return pl.pallas_call(
        functools.partial(_conv_gate_body, seq=seq),
        grid=(n // tm, nc),
        in_specs=[pl.BlockSpec((tm, tc), lambda i, j: (i, j)),
                  pl.BlockSpec((tm, tc), lambda i, j: (i, nc + j)),
                  pl.BlockSpec((tm, tc), lambda i, j: (i, 2 * nc + j)),
                  pl.BlockSpec((halo, tc), prev(nc)),
                  pl.BlockSpec((halo, tc), prev(2 * nc)),
                  pl.BlockSpec((CONV_WIDTH, tc), lambda i, j: (0, j))],
        out_specs=pl.BlockSpec((tm, tc), lambda i, j: (i, j)),
        out_shape=jax.ShapeDtypeStruct((n, d), BF16),
        compiler_params=_cparams(2),
        name="conv_gate",
    )(bch, bch, bch, bch, bch, conv_w)


def _router_body(x_ref, g_ref, wr_ref, xn_ref, idx_ref, wgt_ref, *, n_experts):
    x = x_ref[...]
    y = x * lax.rsqrt(jnp.mean(x * x, axis=-1, keepdims=True) + RMS_EPS) * g_ref[...]
    xn_ref[...] = y
    logits = jnp.dot(y, wr_ref[...], precision=lax.Precision.HIGHEST, preferred_element_type=F32)
    lane = lax.broadcasted_iota(jnp.int32, logits.shape, 1).astype(F32)
    big = float(logits.shape[1])
    lg = jnp.where(lane < n_experts, logits, -jnp.inf)
    m1 = jnp.max(lg, axis=-1, keepdims=True)
    i1 = jnp.min(jnp.where(lg == m1, lane, big), axis=-1, keepdims=True)
    lg2 = jnp.where(lane == i1, -jnp.inf, lg)
    m2 = jnp.max(lg2, axis=-1, keepdims=True)
    i2 = jnp.min(jnp.where(lg2 == m2, lane, big), axis=-1, keepdims=True)
    e = jnp.exp(m2 - m1)
    w1 = 1.0 / (1.0 + e)
    w2 = e * w1
    k = idx_ref.shape[1]
    idx_ref[...] = jnp.where(lane == 0.0, i1, i2)[:, :k].astype(jnp.int32)
    wgt_ref[...] = jnp.where(lane == 0.0, w1, w2)[:, :k]


def _router(h, g, w_router):
    n, d = h.shape
    n_experts = w_router.shape[1]
    tm = _tile(n, NORM_ROW_TILE, 8)
    wr = jnp.pad(w_router, ((0, 0), (0, LANES - n_experts)))
    return pl.pallas_call(
        functools.partial(_router_body, n_experts=n_experts),
        grid=(n // tm,),
        in_specs=[pl.BlockSpec((tm, d), lambda i: (i, 0)),
                  pl.BlockSpec((1, d), lambda i: (0, 0)),
                  pl.BlockSpec((d, LANES), lambda i: (0, 0))],
        out_specs=[pl.BlockSpec((tm, d), lambda i: (i, 0)),
                   pl.BlockSpec((tm, EXPERT_TOP_K), lambda i: (i, 0)),
                   pl.BlockSpec((tm, EXPERT_TOP_K), lambda i: (i, 0))],
        out_shape=[jax.ShapeDtypeStruct((n, d), F32),
                   jax.ShapeDtypeStruct((n, EXPERT_TOP_K), jnp.int32),
                   jax.ShapeDtypeStruct((n, EXPERT_TOP_K), F32)],
        compiler_params=_cparams(1),
        name="router",
    )(h, g.reshape(1, d), wr)


def _group_by_expert(top_idx, n_experts, tm):
    n = top_idx.shape[0]
    n_pairs = n * EXPERT_TOP_K
    e_flat = top_idx.reshape(-1)
    onehot = (e_flat[:, None] == jnp.arange(n_experts, dtype=jnp.int32)[None, :]).astype(jnp.int32)
    csum = jnp.cumsum(onehot, axis=0)
    rank = jnp.sum(onehot * csum, axis=1) - 1
    counts = csum[-1]
    padded = (counts + tm - 1) // tm * tm
    ends = jnp.cumsum(padded)
    starts = ends - padded
    dest = (jnp.sum(onehot * starts[None, :], axis=1) + rank).astype(jnp.int32)
    p_rows = n_pairs + n_experts * tm
    row_src = jnp.zeros((p_rows,), jnp.int32).at[dest].set(
        jnp.arange(n_pairs, dtype=jnp.int32) // EXPERT_TOP_K)
    n_tiles = p_rows // tm
    tile_end = ends // tm
    n_used = tile_end[-1]
    tile_ids = jnp.arange(n_tiles, dtype=jnp.int32)
    expert_of = lambda t: jnp.sum((t[..., None] >= tile_end).astype(jnp.int32), axis=-1)
    tile_expert = jnp.where(tile_ids < n_used, expert_of(tile_ids), expert_of(n_used - 1))
    tile_expert = jnp.minimum(tile_expert, n_experts - 1).astype(jnp.int32)
    return dest, row_src, tile_expert, n_used.reshape(1).astype(jnp.int32)


def _gather_rows_body(src_ref, x_hbm, o_ref, buf_ref, sem):
    tg = buf_ref.shape[0]
    base = pl.program_id(0) * tg

    def row_copy(r):
        return pltpu.make_async_copy(x_hbm.at[pl.ds(src_ref[base + r], 1), :],
                                     buf_ref.at[pl.ds(r, 1), :], sem)

    def start(r, carry):
        row_copy(r).start()
        return carry

    def wait(r, carry):
        row_copy(r).wait()
        return carry

    lax.fori_loop(0, tg, start, 0)
    lax.fori_loop(0, tg, wait, 0)
    o_ref[...] = buf_ref[...].astype(o_ref.dtype)


def _gather_rows(x, row_src):
    p_rows = row_src.shape[0]
    d = x.shape[1]
    tg = _tile(p_rows, GATHER_ROW_TILE, BF16_SUBLANES)
    return pl.pallas_call(
        _gather_rows_body,
        grid_spec=pltpu.PrefetchScalarGridSpec(
            num_scalar_prefetch=1,
            grid=(p_rows // tg,),
            in_specs=[pl.BlockSpec(memory_space=pl.ANY)],
            out_specs=pl.BlockSpec((tg, d), lambda i, src: (i, 0)),
            scratch_shapes=[pltpu.VMEM((tg, d), F32), pltpu.SemaphoreType.DMA(())]),
        out_shape=jax.ShapeDtypeStruct((p_rows, d), BF16),
        compiler_params=_cparams(1),
        name="moe_gather",
    )(row_src, x)


def _first_tile_of_expert(te_ref, i):
    return jnp.logical_or(i == 0, te_ref[i] != te_ref[jnp.maximum(i - 1, 0)])


def _moe_swiglu_body(te_ref, nu_ref, x_ref, wg_ref, wu_ref, o_ref, wgb_ref, wub_ref):
    i = pl.program_id(1)

    @pl.when(_first_tile_of_expert(te_ref, i))
    def _():
        _cast_rows(wg_ref, wgb_ref)
        _cast_rows(wu_ref, wub_ref)

    @pl.when(i < nu_ref[0])
    def _():
        x = x_ref[...]
        g = jnp.dot(x, wgb_ref[...], preferred_element_type=F32)
        u = jnp.dot(x, wub_ref[...], preferred_element_type=F32)
        o_ref[...] = _swiglu_act(g, u).astype(o_ref.dtype)

    @pl.when(i >= nu_ref[0])
    def _():
        o_ref[...] = jnp.zeros(o_ref.shape, o_ref.dtype)


def _moe_down_body(te_ref, nu_ref, x_ref, w_ref, o_ref, wb_ref):
    i = pl.program_id(1)

    @pl.when(_first_tile_of_expert(te_ref, i))
    def _():
        _cast_rows(w_ref, wb_ref)

    @pl.when(i < nu_ref[0])
    def _():
        o_ref[...] = jnp.dot(x_ref[...], wb_ref[...], preferred_element_type=F32)

    @pl.when(i >= nu_ref[0])
    def _():
        o_ref[...] = jnp.zeros(o_ref.shape, o_ref.dtype)


def _moe_swiglu(xg, w_gate_up, tile_expert, n_used, tm):
    p_rows, k = xg.shape
    f = w_gate_up.shape[2] // 2
    tn = _tile(f, COL_TILE // 2)
    nj = f // tn
    last = lambda i, nu: jnp.minimum(i, nu[0] - 1)
    return pl.pallas_call(
        _moe_swiglu_body,
        grid_spec=pltpu.PrefetchScalarGridSpec(
            num_scalar_prefetch=2,
            grid=(nj, p_rows // tm),
            in_specs=[pl.BlockSpec((tm, k), lambda j, i, te, nu: (last(i, nu), 0)),
                      pl.BlockSpec((None, k, tn), lambda j, i, te, nu: (te[i], 0, j)),
                      pl.BlockSpec((None, k, tn), lambda j, i, te, nu: (te[i], 0, nj + j))],
            out_specs=pl.BlockSpec((tm, tn), lambda j, i, te, nu: (i, j)),
            scratch_shapes=[pltpu.VMEM((k, tn), BF16), pltpu.VMEM((k, tn), BF16)]),
        out_shape=jax.ShapeDtypeStruct((p_rows, f), BF16),
        compiler_params=_cparams(2),
        name="moe_swiglu",
    )(tile_expert, n_used, xg, w_gate_up, w_gate_up)


def _moe_down(act, w_down, tile_expert, n_used, tm):
    p_rows, k = act.shape
    n = w_down.shape[2]
    tn = _tile(n, COL_TILE)
    last = lambda i, nu: jnp.minimum(i, nu[0] - 1)
    return pl.pallas_call(
        _moe_down_body,
        grid_spec=pltpu.PrefetchScalarGridSpec(
            num_scalar_prefetch=2,
            grid=(n // tn, p_rows // tm),
            in_specs=[pl.BlockSpec((tm, k), lambda j, i, te, nu: (last(i, nu), 0)),
                      pl.BlockSpec((None, k, tn), lambda j, i, te, nu: (te[i], 0, j))],
            out_specs=pl.BlockSpec((tm, tn), lambda j, i, te, nu: (i, j)),
            scratch_shapes=[pltpu.VMEM((k, tn), BF16)]),
        out_shape=jax.ShapeDtypeStruct((p_rows, n), F32),
        compiler_params=_cparams(2),
        name="moe_down",
    )(tile_expert, n_used, act, w_down)


def _combine_body(dest_ref, h_ref, w_ref, y_hbm, o_ref, a_ref, b_ref, sem_a, sem_b):
    tc = h_ref.shape[0]
    base = pl.program_id(0) * tc

    def copies(r):
        pair = (base + r) * EXPERT_TOP_K
        return (pltpu.make_async_copy(y_hbm.at[pl.ds(dest_ref[pair], 1), :],
                                      a_ref.at[pl.ds(r, 1), :], sem_a),
                pltpu.make_async_copy(y_hbm.at[pl.ds(dest_ref[pair + 1], 1), :],
                                      b_ref.at[pl.ds(r, 1), :], sem_b))

    def start(r, carry):
        for cp in copies(r):
            cp.start()
        return carry

    def wait(r, carry):
        for cp in copies(r):
            cp.wait()
        return carry

    lax.fori_loop(0, tc, start, 0)
    lax.fori_loop(0, tc, wait, 0)
    w = w_ref[...]
    o_ref[...] = h_ref[...] + (w[:, 0:1] * a_ref[...] + w[:, 1:2] * b_ref[...])


def _combine(h, top_w, y, dest):
    n, d = h.shape
    tc = _tile(n, GATHER_ROW_TILE, 8)
    return pl.pallas_call(
        _combine_body,
        grid_spec=pltpu.PrefetchScalarGridSpec(
            num_scalar_prefetch=1,
            grid=(n // tc,),
            in_specs=[pl.BlockSpec((tc, d), lambda i, dst: (i, 0)),
                      pl.BlockSpec((tc, EXPERT_TOP_K), lambda i, dst: (i, 0)),
                      pl.BlockSpec(memory_space=pl.ANY)],
            out_specs=pl.BlockSpec((tc, d), lambda i, dst: (i, 0)),
            scratch_shapes=[pltpu.VMEM((tc, d), F32), pltpu.VMEM((tc, d), F32),
                            pltpu.SemaphoreType.DMA(()), pltpu.SemaphoreType.DMA(())]),
        out_shape=jax.ShapeDtypeStruct((n, d), F32),
        compiler_params=_cparams(1),
        name="moe_combine",
    )(dest, h, top_w, y)


def kernel(x, attn_norm, w_qkv, q_norm, k_norm, w_attn_out, ffn_norm, w_ffn_gate_up, w_ffn_down,
           conv_norm, w_conv_in, conv_w, w_conv_out, moe_norm, w_router, w_expert_gate_up,
           w_expert_down):
    batch, seq, d = x.shape
    head_dim = q_norm.shape[-1]
    n_heads = d // head_dim
    n_experts = w_router.shape[-1]
    depth = attn_norm.shape[0] + conv_norm.shape[0]
    h = x.reshape(batch * seq, d)

    for layer in range(depth):
        j = layer // 2
        if layer % 2 == 0:
            gain = jnp.concatenate([jnp.tile(q_norm[j] * (1.0 / math.sqrt(head_dim)), n_heads),
                                    jnp.tile(k_norm[j], n_heads),
                                    jnp.ones((d,), F32)]).reshape(1, 3 * d)
            xn = _rmsnorm(h, attn_norm[j])
            qkv = _mm(xn, w_qkv[j], qk_gain=gain, n_norm_cols=2 * d, head_dim=head_dim)
            o = _moba_attention(qkv, batch, seq, n_heads, head_dim)
            h = _mm(o, w_attn_out[j], residual=h)
            xn = _rmsnorm(h, ffn_norm[j])
            act = _mm_swiglu(xn, w_ffn_gate_up[j])
            h = _mm_ktiled_residual(act, w_ffn_down[j], h)
        else:
            xn = _rmsnorm(h, conv_norm[j])
            bch = _mm(xn, w_conv_in[j])
            z = _conv_gate(bch, conv_w[j], seq)
            h = _mm(z, w_conv_out[j], residual=h)
            xn32, top_idx, top_w = _router(h, moe_norm[j], w_router[j])
            tm = _tile(h.shape[0], MOE_ROW_TILE, BF16_SUBLANES)
            dest, row_src, tile_expert, n_used = _group_by_expert(top_idx, n_experts, tm)
            xg = _gather_rows(xn32, row_src)
            act = _moe_swiglu(xg, w_expert_gate_up[j], tile_expert, n_used, tm)
            y = _moe_down(act, w_expert_down[j], tile_expert, n_used, tm)
            h = _combine(h, top_w, y, dest)
    return h.reshape(batch, seq, d)
```

```python
import functools
import math

import numpy as np
import jax
import jax.numpy as jnp
from jax import lax
from jax.experimental import pallas as pl
from jax.experimental.pallas import tpu as pltpu

F32 = jnp.float32
BF16 = jnp.bfloat16

MOBA_BLOCK = 256
MOBA_TOP_K = 3
EXPERT_TOP_K = 2
CONV_WIDTH = 3
RMS_EPS = 1e-6
NEG = -1e30

LANES = 128
BF16_SUBLANES = 16
V7X_VMEM_BYTES = 64 * 1024 * 1024
VMEM_LIMIT_BYTES = V7X_VMEM_BYTES - 8 * 1024 * 1024

ROW_TILE = 1024
COL_TILE = 512
MOE_UP_ROW_TILE = 1024
MOE_DOWN_ROW_TILE = 512
MOE_SUB_ROWS = 256
NORM_ROW_TILE = 256
GATHER_ROW_TILE = 256
SLAB_ROWS = 40
CAST_ROWS = 256


def _tile(dim, pref, mult=LANES):
    if dim <= pref:
        return dim
    t = pref - pref % mult
    while t >= mult:
        if dim % t == 0:
            return t
        t -= mult
    return dim


def _cparams(n_axes):
    return pltpu.CompilerParams(dimension_semantics=("arbitrary",) * n_axes,
                                vmem_limit_bytes=VMEM_LIMIT_BYTES)


def _cast_rows(w_ref, wb_ref):
    k = w_ref.shape[0]
    rows = _tile(k, CAST_ROWS, BF16_SUBLANES)

    def step(r, carry):
        sl = pl.ds(pl.multiple_of(r * rows, rows), rows)
        wb_ref[sl, :] = w_ref[sl, :].astype(wb_ref.dtype)
        return carry

    lax.fori_loop(0, k // rows, step, 0)


def _rmsnorm_body(x_ref, g_ref, o_ref):
    x = x_ref[...]
    inv = lax.rsqrt(jnp.mean(x * x, axis=-1, keepdims=True) + RMS_EPS)
    o_ref[...] = (x * inv * g_ref[...]).astype(o_ref.dtype)


def _rmsnorm(x, g):
    n, d = x.shape
    tm = _tile(n, NORM_ROW_TILE, BF16_SUBLANES)
    return pl.pallas_call(
        _rmsnorm_body,
        grid=(n // tm,),
        in_specs=[pl.BlockSpec((tm, d), lambda i: (i, 0)),
                  pl.BlockSpec((1, d), lambda i: (0, 0))],
        out_specs=pl.BlockSpec((tm, d), lambda i: (i, 0)),
        out_shape=jax.ShapeDtypeStruct((n, d), BF16),
        compiler_params=_cparams(1),
        name="rmsnorm",
    )(x, g.reshape(1, d))


def _add_rmsnorm_body(x_ref, dx_ref, g_ref, h_ref, o_ref):
    x = x_ref[...] + dx_ref[...]
    h_ref[...] = x
    inv = lax.rsqrt(jnp.mean(x * x, axis=-1, keepdims=True) + RMS_EPS)
    o_ref[...] = (x * inv * g_ref[...]).astype(o_ref.dtype)


def _add_rmsnorm(x, dx, g):
    n, d = x.shape
    tm = _tile(n, NORM_ROW_TILE, BF16_SUBLANES)
    row = pl.BlockSpec((tm, d), lambda i: (i, 0))
    return pl.pallas_call(
        _add_rmsnorm_body,
        grid=(n // tm,),
        in_specs=[row, row, pl.BlockSpec((1, d), lambda i: (0, 0))],
        out_specs=[row, row],
        out_shape=[jax.ShapeDtypeStruct((n, d), F32), jax.ShapeDtypeStruct((n, d), BF16)],
        compiler_params=_cparams(1),
        name="add_rmsnorm",
    )(x, dx, g.reshape(1, d))


def _mm_plain_body(x_ref, w_ref, o_ref, wb_ref):
    @pl.when(pl.program_id(1) == 0)
    def _():
        _cast_rows(w_ref, wb_ref)

    o_ref[...] = jnp.dot(x_ref[...], wb_ref[...], preferred_element_type=F32).astype(o_ref.dtype)


def _mm_residual_body(x_ref, w_ref, r_ref, o_ref, wb_ref):
    @pl.when(pl.program_id(1) == 0)
    def _():
        _cast_rows(w_ref, wb_ref)

    o_ref[...] = r_ref[...] + jnp.dot(x_ref[...], wb_ref[...], preferred_element_type=F32)


def _mm_qknorm_body(x_ref, w_ref, g_ref, o_ref, wb_ref, *, n_norm_tiles, head_dim):
    j = pl.program_id(0)

    @pl.when(pl.program_id(1) == 0)
    def _():
        _cast_rows(w_ref, wb_ref)

    acc = jnp.dot(x_ref[...], wb_ref[...], preferred_element_type=F32)

    @pl.when(j < n_norm_tiles)
    def _():
        for hh in range(acc.shape[1] // head_dim):
            sl = slice(hh * head_dim, (hh + 1) * head_dim)
            a = acc[:, sl]
            inv = lax.rsqrt(jnp.mean(a * a, axis=-1, keepdims=True) + RMS_EPS)
            o_ref[:, sl] = (a * inv * g_ref[:, sl]).astype(o_ref.dtype)

    @pl.when(j >= n_norm_tiles)
    def _():
        o_ref[...] = acc.astype(o_ref.dtype)


def _swiglu_act(g, u):
    return g * (1.0 / (1.0 + jnp.exp(-g))) * u


def _mm_swiglu_body(x_ref, wg_ref, wu_ref, o_ref, wgb_ref, wub_ref):
    @pl.when(pl.program_id(1) == 0)
    def _():
        _cast_rows(wg_ref, wgb_ref)
        _cast_rows(wu_ref, wub_ref)

    x = x_ref[...]
    g = jnp.dot(x, wgb_ref[...], preferred_element_type=F32)
    u = jnp.dot(x, wub_ref[...], preferred_element_type=F32)
    o_ref[...] = _swiglu_act(g, u).astype(o_ref.dtype)


def _mm(x, w, *, out_dtype=BF16, residual=None, qk_gain=None, n_norm_cols=0, head_dim=LANES):
    m, k = x.shape
    n = w.shape[1]
    tm = _tile(m, ROW_TILE, BF16_SUBLANES)
    tn = _tile(n, COL_TILE)
    grid = (n // tn, m // tm)
    in_specs = [pl.BlockSpec((tm, k), lambda j, i: (i, 0)),
                pl.BlockSpec((k, tn), lambda j, i: (0, j))]
    args = [x, w]
    if residual is not None:
        body = _mm_residual_body
        in_specs.append(pl.BlockSpec((tm, tn), lambda j, i: (i, j)))
        args.append(residual)
        out_dtype = F32
        name = "mm_residual"
    elif qk_gain is not None:
        assert n_norm_cols % tn == 0 and tn % head_dim == 0
        body = functools.partial(_mm_qknorm_body, n_norm_tiles=n_norm_cols // tn, head_dim=head_dim)
        in_specs.append(pl.BlockSpec((1, tn), lambda j, i: (0, j)))
        args.append(qk_gain)
        name = "mm_qknorm"
    else:
        body = _mm_plain_body
        name = "mm_plain"
    return pl.pallas_call(
        body,
        grid=grid,
        in_specs=in_specs,
        out_specs=pl.BlockSpec((tm, tn), lambda j, i: (i, j)),
        out_shape=jax.ShapeDtypeStruct((m, n), out_dtype),
        scratch_shapes=[pltpu.VMEM((k, tn), BF16)],
        compiler_params=_cparams(2),
        name=name,
    )(*args)


def _mm_swiglu(x, w_gate_up):
    m, k = x.shape
    f = w_gate_up.shape[1] // 2
    tm = _tile(m, ROW_TILE, BF16_SUBLANES)
    tn = _tile(f, COL_TILE // 2)
    nj = f // tn
    return pl.pallas_call(
        _mm_swiglu_body,
        grid=(nj, m // tm),
        in_specs=[pl.BlockSpec((tm, k), lambda j, i: (i, 0)),
                  pl.BlockSpec((k, tn), lambda j, i: (0, j)),
                  pl.BlockSpec((k, tn), lambda j, i: (0, nj + j))],
        out_specs=pl.BlockSpec((tm, tn), lambda j, i: (i, j)),
        out_shape=jax.ShapeDtypeStruct((m, f), BF16),
        scratch_shapes=[pltpu.VMEM((k, tn), BF16), pltpu.VMEM((k, tn), BF16)],
        compiler_params=_cparams(2),
        name="mm_swiglu",
    )(x, w_gate_up, w_gate_up)


def _mm_ktiled_body(x_ref, w_ref, o_ref):
    kk = pl.program_id(2)
    part = jnp.dot(x_ref[...], w_ref[...].astype(BF16), preferred_element_type=F32)

    @pl.when(kk == 0)
    def _():
        o_ref[...] = part

    @pl.when(kk > 0)
    def _():
        o_ref[...] += part


def _mm_ktiled(x, w):
    m, k = x.shape
    n = w.shape[1]
    tm = _tile(m, 2 * ROW_TILE, BF16_SUBLANES)
    tn = _tile(n, 2 * COL_TILE)
    tk = _tile(k, 1024)
    return pl.pallas_call(
        _mm_ktiled_body,
        grid=(m // tm, n // tn, k // tk),
        in_specs=[pl.BlockSpec((tm, tk), lambda i, j, kk: (i, kk)),
                  pl.BlockSpec((tk, tn), lambda i, j, kk: (kk, j))],
        out_specs=pl.BlockSpec((tm, tn), lambda i, j, kk: (i, j)),
        out_shape=jax.ShapeDtypeStruct((m, n), F32),
        compiler_params=_cparams(3),
        name="mm_ktiled",
    )(x, w)


LOG2E = math.log2(math.e)
SLOPE_TERMS = 3


def _col_reduce(x, op):
    rows = x.shape[0]
    while rows > 8 and rows % 16 == 0:
        rows //= 2
        x = op(x[:rows, :], x[rows:, :])
    red = jnp.max if op is jnp.maximum else jnp.sum
    return red(x, axis=0, keepdims=True)


def _attn_body(qc_ref, q_ref, k_ref, v_ref, o_ref, ka_ref, qa_ref, vt_ref, *, n_blocks, top_k):
    blk = MOBA_BLOCK
    blk_shift = int(math.log2(blk))
    s_len = n_blocks * blk
    hd = q_ref.shape[1]
    nt_dims = (((1,), (1,)), ((), ()))

    km = jnp.concatenate(
        [jnp.mean(k_ref[n * blk:(n + 1) * blk, :].astype(F32), axis=0, keepdims=True)
         for n in range(n_blocks)], axis=0)
    km_hi = km.astype(BF16)
    km_lo = (km - km_hi.astype(F32)).astype(BF16)
    km2 = jnp.concatenate([km_hi, km_lo], axis=1)
    q = q_ref[...]
    q2 = jnp.concatenate([q, q], axis=1)
    gate = lax.dot_general(km2, q2, nt_dims, preferred_element_type=F32)

    n_iota = lax.broadcasted_iota(jnp.int32, (n_blocks, s_len), 0)
    q_blk = lax.shift_right_logical(lax.broadcasted_iota(jnp.int32, (n_blocks, s_len), 1), blk_shift)
    valid = n_iota < q_blk
    g = jnp.where(valid, gate, -jnp.inf)
    rank = jnp.zeros((n_blocks, s_len), jnp.int32)
    for mm in range(n_blocks):
        gm = g[mm:mm + 1, :]
        tie = jnp.where(n_iota > mm, 1, 0)
        rank = rank + jnp.where(gm > g, 1, jnp.where(gm == g, tie, 0))
    sel = jnp.where(valid, jnp.where(rank < top_k, 0.0, NEG),
                    jnp.where(n_iota == q_blk, 0.0, NEG))
    sel_t = jnp.concatenate([sel, jnp.zeros((LANES - n_blocks, s_len), F32)], axis=0).T

    qa_ref[:, :hd] = q
    qa_ref[:, hd:] = (sel_t + qc_ref[...]).astype(BF16)
    t = lax.broadcasted_iota(jnp.int32, (s_len, LANES), 0)
    lane = lax.broadcasted_iota(jnp.int32, (s_len, LANES), 1)
    t_blk = lax.shift_right_logical(t, blk_shift)
    t_off = jnp.bitwise_and(t, blk - 1)
    k_cols = jnp.where(lane < n_blocks, jnp.where(lane == t_blk, 1.0, 0.0),
                       jnp.where(lane < n_blocks + SLOPE_TERMS, t_off.astype(F32),
                                 jnp.where(lane < n_blocks + 2 * SLOPE_TERMS, t_blk.astype(F32), 0.0)))
    ka_ref[:, :hd] = k_ref[...]
    ka_ref[:, hd:] = k_cols.astype(BF16)

    vt_ref[...] = v_ref[...].astype(F32).T.astype(vt_ref.dtype)

    r = lax.broadcasted_iota(jnp.int32, (blk, blk), 1)
    c = lax.broadcasted_iota(jnp.int32, (blk, blk), 0)
    causal = c <= r

    for i in range(n_blocks):
        sl = slice(i * blk, (i + 1) * blk)
        n_keys = (i + 1) * blk
        s = lax.dot_general(ka_ref[:n_keys, :], qa_ref[sl, :], nt_dims,
                            preferred_element_type=F32)
        s_own = jnp.where(causal, s[i * blk:, :], NEG)
        m = _col_reduce(s_own, jnp.maximum)
        if i > 0:
            s_past = s[:i * blk, :]
            m = jnp.maximum(m, _col_reduce(s_past, jnp.maximum))
        p_own = jnp.exp2(s_own - m)
        l = _col_reduce(p_own, jnp.add)
        p = p_own.astype(BF16)
        if i > 0:
            p_past = jnp.exp2(s_past - m)
            l = l + _col_reduce(p_past, jnp.add)
            p = jnp.concatenate([p_past.astype(BF16), p], axis=0)
        acc = jnp.dot(vt_ref[:, :n_keys], p, preferred_element_type=F32)
        o_ref[sl, :] = (acc * (1.0 / l)).T.astype(o_ref.dtype)


def _moba_attention(qkv, batch, seq, n_heads, head_dim):
    assert seq % MOBA_BLOCK == 0 and head_dim == LANES
    n_blocks = seq // MOBA_BLOCK
    assert n_blocks + 2 * SLOPE_TERMS <= LANES
    top_k = min(MOBA_TOP_K, n_blocks - 1)
    rest = jnp.asarray(np.power(2.0, -8.0 * np.arange(1, n_heads + 1) / n_heads).astype(np.float32)) * LOG2E
    terms = []
    for _ in range(SLOPE_TERMS):
        term = rest.astype(BF16).astype(F32)
        terms.append(term)
        rest = rest - term
    cols = jnp.stack(terms + [MOBA_BLOCK * term for term in terms], axis=1)
    q_cols = jnp.zeros((n_heads, 1, LANES), F32).at[:, 0, n_blocks:n_blocks + 2 * SLOPE_TERMS].set(cols)
    blk_spec = lambda off: pl.BlockSpec((seq, head_dim), lambda b, h: (b, off + h))
    return pl.pallas_call(
        functools.partial(_attn_body, n_blocks=n_blocks, top_k=top_k),
        grid=(batch, n_heads),
        in_specs=[pl.BlockSpec((None, 1, LANES), lambda b, h: (h, 0, 0)),
                  blk_spec(0), blk_spec(n_heads), blk_spec(2 * n_heads)],
        out_specs=pl.BlockSpec((seq, head_dim), lambda b, h: (b, h)),
        out_shape=jax.ShapeDtypeStruct((batch * seq, n_heads * head_dim), BF16),
        scratch_shapes=[pltpu.VMEM((seq, 2 * head_dim), BF16),
                        pltpu.VMEM((seq, 2 * head_dim), BF16),
                        pltpu.VMEM((head_dim, seq), BF16)],
        compiler_params=_cparams(2),
        name="moba_attention",
    )(q_cols, qkv, qkv, qkv)


def _conv_gate_body(b_ref, c_ref, h_ref, cp_ref, hp_ref, w_ref, o_ref, *, seq):
    tm = c_ref.shape[0]
    u = c_ref[...].astype(F32) * h_ref[...].astype(F32)
    keep = jnp.where((pl.program_id(0) * tm) % seq == 0, 0.0, 1.0)
    up = cp_ref[...].astype(F32) * hp_ref[...].astype(F32) * keep
    last = up.shape[0] - 1
    row = lax.broadcasted_iota(jnp.int32, u.shape, 0)
    u1 = jnp.where(row == 0, up[last:last + 1, :], pltpu.roll(u, 1, 0))
    u2 = jnp.where(row == 0, up[last - 1:last, :],
                   jnp.where(row == 1, up[last:last + 1, :], pltpu.roll(u, 2, 0)))
    y = w_ref[0:1, :] * u2 + w_ref[1:2, :] * u1 + w_ref[2:3, :] * u
    o_ref[...] = (b_ref[...].astype(F32) * y).astype(o_ref.dtype)


def _conv_gate(bch, conv_w, seq):
    n, d3 = bch.shape
    d = d3 // 3
    tm = _tile(seq, 512, BF16_SUBLANES)
    tc = _tile(d, 1024)
    nc = d // tc
    halo = BF16_SUBLANES
    hb = tm // halo
    prev = lambda off: (lambda i, j: (jnp.maximum(i * hb - 1, 0), off + j))
    return pl.pallas_call(
        functools.partial(_conv_gate_body, seq=seq),
        grid=(n // tm, nc),
        in_specs=[pl.BlockSpec((tm, tc), lambda i, j: (i, j)),
                  pl.BlockSpec((tm, tc), lambda i, j: (i, nc + j)),
                  pl.BlockSpec((tm, tc), lambda i, j: (i, 2 * nc + j)),
                  pl.BlockSpec((halo, tc), prev(nc)),
                  pl.BlockSpec((halo, tc), prev(2 * nc)),
                  pl.BlockSpec((CONV_WIDTH, tc), lambda i, j: (0, j))],
        out_specs=pl.BlockSpec((tm, tc), lambda i, j: (i, j)),
        out_shape=jax.ShapeDtypeStruct((n, d), BF16),
        compiler_params=_cparams(2),
        name="conv_gate",
    )(bch, bch, bch, bch, bch, conv_w)


def _router_body(x_ref, g_ref, wr_ref, xs_ref, idx_ref, wgt_ref, *, n_experts):
    x = x_ref[...]
    y = x * lax.rsqrt(jnp.mean(x * x, axis=-1, keepdims=True) + RMS_EPS) * g_ref[...]
    tm, d = y.shape
    for s in range(d // LANES):
        xs_ref[pl.ds(s, tm, stride=SLAB_ROWS), :] = y[:, s * LANES:(s + 1) * LANES]
    for s in range(d // LANES, SLAB_ROWS):
        xs_ref[pl.ds(s, tm, stride=SLAB_ROWS), :] = jnp.zeros((tm, LANES), F32)
    logits = jnp.dot(y, wr_ref[...], precision=lax.Precision.HIGHEST, preferred_element_type=F32)
    lane = lax.broadcasted_iota(jnp.int32, logits.shape, 1).astype(F32)
    big = float(logits.shape[1])
    lg = jnp.where(lane < n_experts, logits, -jnp.inf)
    m1 = jnp.max(lg, axis=-1, keepdims=True)
    i1 = jnp.min(jnp.where(lg == m1, lane, big), axis=-1, keepdims=True)
    lg2 = jnp.where(lane == i1, -jnp.inf, lg)
    m2 = jnp.max(lg2, axis=-1, keepdims=True)
    i2 = jnp.min(jnp.where(lg2 == m2, lane, big), axis=-1, keepdims=True)
    e = jnp.exp(m2 - m1)
    w1 = 1.0 / (1.0 + e)
    w2 = e * w1
    k = idx_ref.shape[1]
    idx_ref[...] = jnp.where(lane == 0.0, i1, i2)[:, :k].astype(jnp.int32)
    wgt_ref[...] = jnp.where(lane == 0.0, w1, w2)[:, :k]


def _router(h, g, w_router):
    n, d = h.shape
    assert d % LANES == 0 and d // LANES <= SLAB_ROWS
    n_experts = w_router.shape[1]
    tm = _tile(n, NORM_ROW_TILE, 8)
    wr = jnp.pad(w_router, ((0, 0), (0, LANES - n_experts)))
    return pl.pallas_call(
        functools.partial(_router_body, n_experts=n_experts),
        grid=(n // tm,),
        in_specs=[pl.BlockSpec((tm, d), lambda i: (i, 0)),
                  pl.BlockSpec((1, d), lambda i: (0, 0)),
                  pl.BlockSpec((d, LANES), lambda i: (0, 0))],
        out_specs=[pl.BlockSpec((tm * SLAB_ROWS, LANES), lambda i: (i, 0)),
                   pl.BlockSpec((tm, EXPERT_TOP_K), lambda i: (i, 0)),
                   pl.BlockSpec((tm, EXPERT_TOP_K), lambda i: (i, 0))],
        out_shape=[jax.ShapeDtypeStruct((n * SLAB_ROWS, LANES), F32),
                   jax.ShapeDtypeStruct((n, EXPERT_TOP_K), jnp.int32),
                   jax.ShapeDtypeStruct((n, EXPERT_TOP_K), F32)],
        compiler_params=_cparams(1),
        name="router",
    )(h, g.reshape(1, d), wr)


def _group_by_expert(top_idx, n_experts, align):
    n = top_idx.shape[0]
    n_pairs = n * EXPERT_TOP_K
    e_flat = top_idx.reshape(-1)
    onehot = (e_flat[:, None] == jnp.arange(n_experts, dtype=jnp.int32)[None, :]).astype(jnp.int32)
    csum = jnp.cumsum(onehot, axis=0)
    rank = jnp.sum(onehot * csum, axis=1) - 1
    counts = csum[-1]
    padded = (counts + align - 1) // align * align
    ends = jnp.cumsum(padded)
    starts = ends - padded
    dest = (jnp.sum(onehot * starts[None, :], axis=1) + rank).astype(jnp.int32)
    p_rows = n_pairs + n_experts * align
    row_src = jnp.zeros((p_rows,), jnp.int32).at[dest].set(
        jnp.arange(n_pairs, dtype=jnp.int32) // EXPERT_TOP_K)
    return dest, row_src, (starts, starts + counts, ends)


def _tile_table(groups, p_rows, tm):
    starts, valid_ends, ends = groups
    n_experts = ends.shape[0]
    tile_start = jnp.arange(p_rows // tm, dtype=jnp.int32) * tm
    before = jnp.sum((tile_start[:, None] >= ends[None, :]).astype(jnp.int32), axis=1)
    live = before < n_experts
    n_live = ends[-1] // tm
    expert = jnp.minimum(before, n_experts - 1)
    rows = jnp.where(live, jnp.clip(valid_ends[expert] - tile_start, 0, tm), 0)
    expert = jnp.where(live, expert, expert[jnp.maximum(n_live - 1, 0)])
    return expert.astype(jnp.int32), rows.astype(jnp.int32), n_live.reshape(1).astype(jnp.int32)


def _gather_rows_body(src_ref, x_hbm, o_ref, buf_ref, sem):
    tg, d = o_ref.shape
    n_slab = d // LANES
    base = pl.program_id(0) * tg

    def token_copy(r):
        src = pl.multiple_of(src_ref[base + r] * SLAB_ROWS, SLAB_ROWS)
        dst = pl.multiple_of(r * SLAB_ROWS, SLAB_ROWS)
        return pltpu.make_async_copy(x_hbm.at[pl.ds(src, n_slab), :],
                                     buf_ref.at[pl.ds(dst, n_slab), :], sem)

    def start(r, carry):
        token_copy(r).start()
        return carry

    def wait(r, carry):
        token_copy(r).wait()
        return carry

    lax.fori_loop(0, tg, start, 0)
    lax.fori_loop(0, tg, wait, 0)
    for s in range(n_slab):
        o_ref[:, s * LANES:(s + 1) * LANES] = buf_ref[pl.ds(s, tg, stride=SLAB_ROWS), :].astype(o_ref.dtype)


def _gather_rows(xs, row_src, d):
    p_rows = row_src.shape[0]
    tg = _tile(p_rows, GATHER_ROW_TILE, BF16_SUBLANES)
    return pl.pallas_call(
        _gather_rows_body,
        grid_spec=pltpu.PrefetchScalarGridSpec(
            num_scalar_prefetch=1,
            grid=(p_rows // tg,),
            in_specs=[pl.BlockSpec(memory_space=pl.ANY)],
            out_specs=pl.BlockSpec((tg, d), lambda i, src: (i, 0)),
            scratch_shapes=[pltpu.VMEM((tg * SLAB_ROWS, LANES), F32), pltpu.SemaphoreType.DMA(())]),
        out_shape=jax.ShapeDtypeStruct((p_rows, d), BF16),
        compiler_params=_cparams(1),
        name="moe_gather",
    )(row_src, xs)


def _first_tile_of_expert(te_ref, i):
    return jnp.logical_or(i == 0, te_ref[i] != te_ref[jnp.maximum(i - 1, 0)])


def _for_valid_rows(rows, o_ref, compute):
    tm = o_ref.shape[0]
    sub = min(MOE_SUB_ROWS, tm)
    for m in range(sub, tm + 1, sub):
        @pl.when(jnp.logical_and(rows > m - sub, rows <= m))
        def _(m=m):
            o_ref[:m, :] = compute(m)
            if m < tm:
                o_ref[m:, :] = jnp.zeros((tm - m, o_ref.shape[1]), o_ref.dtype)

    @pl.when(rows <= 0)
    def _():
        o_ref[...] = jnp.zeros(o_ref.shape, o_ref.dtype)


def _moe_swiglu_body(te_ref, tr_ref, nl_ref, x_ref, wg_ref, wu_ref, o_ref, wgb_ref, wub_ref):
    i = pl.program_id(1)

    @pl.when(_first_tile_of_expert(te_ref, i))
    def _():
        _cast_rows(wg_ref, wgb_ref)
        _cast_rows(wu_ref, wub_ref)

    def compute(m):
        x = x_ref[:m, :]
        g = jnp.dot(x, wgb_ref[...], preferred_element_type=F32)
        u = jnp.dot(x, wub_ref[...], preferred_element_type=F32)
        return _swiglu_act(g, u).astype(o_ref.dtype)

    _for_valid_rows(tr_ref[i], o_ref, compute)


def _moe_down_body(te_ref, tr_ref, nl_ref, x_ref, w_ref, o_ref, wb_ref):
    i = pl.program_id(1)

    @pl.when(_first_tile_of_expert(te_ref, i))
    def _():
        _cast_rows(w_ref, wb_ref)

    def compute(m):
        return jnp.dot(x_ref[:m, :], wb_ref[...], preferred_element_type=F32)

    _for_valid_rows(tr_ref[i], o_ref, compute)


def _live_tile(i, nl):
    return jnp.minimum(i, nl[0] - 1)


def _moe_swiglu(xg, w_gate_up, groups, tm):
    p_rows, k = xg.shape
    f = w_gate_up.shape[2] // 2
    tn = _tile(f, COL_TILE // 2)
    nj = f // tn
    tile_expert, tile_rows, n_live = _tile_table(groups, p_rows, tm)
    return pl.pallas_call(
        _moe_swiglu_body,
        grid_spec=pltpu.PrefetchScalarGridSpec(
            num_scalar_prefetch=3,
            grid=(nj, p_rows // tm),
            in_specs=[pl.BlockSpec((tm, k), lambda j, i, te, tr, nl: (_live_tile(i, nl), 0)),
                      pl.BlockSpec((None, k, tn), lambda j, i, te, tr, nl: (te[i], 0, j)),
                      pl.BlockSpec((None, k, tn), lambda j, i, te, tr, nl: (te[i], 0, nj + j))],
            out_specs=pl.BlockSpec((tm, tn), lambda j, i, te, tr, nl: (i, j)),
            scratch_shapes=[pltpu.VMEM((k, tn), BF16), pltpu.VMEM((k, tn), BF16)]),
        out_shape=jax.ShapeDtypeStruct((p_rows, f), BF16),
        compiler_params=_cparams(2),
        name="moe_swiglu",
    )(tile_expert, tile_rows, n_live, xg, w_gate_up, w_gate_up)


def _moe_down(act, w_down, groups, tm):
    p_rows, k = act.shape
    n = w_down.shape[2]
    tn = _tile(n, COL_TILE)
    tile_expert, tile_rows, n_live = _tile_table(groups, p_rows, tm)
    return pl.pallas_call(
        _moe_down_body,
        grid_spec=pltpu.PrefetchScalarGridSpec(
            num_scalar_prefetch=3,
            grid=(n // tn, p_rows // tm),
            in_specs=[pl.BlockSpec((tm, k), lambda j, i, te, tr, nl: (_live_tile(i, nl), 0)),
                      pl.BlockSpec((None, k, tn), lambda j, i, te, tr, nl: (te[i], 0, j))],
            out_specs=pl.BlockSpec((tm, tn), lambda j, i, te, tr, nl: (i, j)),
            scratch_shapes=[pltpu.VMEM((k, tn), BF16)]),
        out_shape=jax.ShapeDtypeStruct((p_rows, n), F32),
        compiler_params=_cparams(2),
        name="moe_down",
    )(tile_expert, tile_rows, n_live, act, w_down)


def _combine_body(dest_ref, h_ref, w_ref, y_hbm, o_ref, a_ref, b_ref, sem_a, sem_b):
    tc = h_ref.shape[0]
    base = pl.program_id(0) * tc

    def copies(r):
        pair = (base + r) * EXPERT_TOP_K
        return (pltpu.make_async_copy(y_hbm.at[pl.ds(dest_ref[pair], 1), :],
                                      a_ref.at[pl.ds(r, 1), :], sem_a),
                pltpu.make_async_copy(y_hbm.at[pl.ds(dest_ref[pair + 1], 1), :],
                                      b_ref.at[pl.ds(r, 1), :], sem_b))

    def start(r, carry):
        for cp in copies(r):
            cp.start()
        return carry

    def wait(r, carry):
        for cp in copies(r):
            cp.wait()
        return carry

    lax.fori_loop(0, tc, start, 0)
    lax.fori_loop(0, tc, wait, 0)
    w = w_ref[...]
    o_ref[...] = h_ref[...] + (w[:, 0:1] * a_ref[...] + w[:, 1:2] * b_ref[...])


def _combine(h, top_w, y, dest):
    n, d = h.shape
    tc = _tile(n, GATHER_ROW_TILE, 8)
    return pl.pallas_call(
        _combine_body,
        grid_spec=pltpu.PrefetchScalarGridSpec(
            num_scalar_prefetch=1,
            grid=(n // tc,),
            in_specs=[pl.BlockSpec((tc, d), lambda i, dst: (i, 0)),
                      pl.BlockSpec((tc, EXPERT_TOP_K), lambda i, dst: (i, 0)),
                      pl.BlockSpec(memory_space=pl.ANY)],
            out_specs=pl.BlockSpec((tc, d), lambda i, dst: (i, 0)),
            scratch_shapes=[pltpu.VMEM((tc, d), F32), pltpu.VMEM((tc, d), F32),
                            pltpu.SemaphoreType.DMA(()), pltpu.SemaphoreType.DMA(())]),
        out_shape=jax.ShapeDtypeStruct((n, d), F32),
        compiler_params=_cparams(1),
        name="moe_combine",
    )(dest, h, top_w, y)


def kernel(x, attn_norm, w_qkv, q_norm, k_norm, w_attn_out, ffn_norm, w_ffn_gate_up, w_ffn_down,
           conv_norm, w_conv_in, conv_w, w_conv_out, moe_norm, w_router, w_expert_gate_up,
           w_expert_down):
    batch, seq, d = x.shape
    head_dim = q_norm.shape[-1]
    n_heads = d // head_dim
    n_experts = w_router.shape[-1]
    assert attn_norm.shape[0] == 1 and conv_norm.shape[0] == 1, "two-layer trunk: attention, then conv"
    n_tok = batch * seq
    h = x.reshape(n_tok, d)

    gain = jnp.concatenate([jnp.tile(q_norm[0] * (LOG2E / math.sqrt(head_dim)), n_heads),
                            jnp.tile(k_norm[0], n_heads),
                            jnp.ones((d,), F32)]).reshape(1, 3 * d)
    xn = _rmsnorm(h, attn_norm[0])
    qkv = _mm(xn, w_qkv[0], qk_gain=gain, n_norm_cols=2 * d, head_dim=head_dim)
    o = _moba_attention(qkv, batch, seq, n_heads, head_dim)
    h = _mm(o, w_attn_out[0], residual=h)
    xn = _rmsnorm(h, ffn_norm[0])
    act = _mm_swiglu(xn, w_ffn_gate_up[0])
    ffn = _mm_ktiled(act, w_ffn_down[0])

    h, xn = _add_rmsnorm(h, ffn, conv_norm[0])
    bch = _mm(xn, w_conv_in[0])
    z = _conv_gate(bch, conv_w[0], seq)
    h = _mm(z, w_conv_out[0], residual=h)
    xs, top_idx, top_w = _router(h, moe_norm[0], w_router[0])
    tm_up = _tile(n_tok, MOE_UP_ROW_TILE, BF16_SUBLANES)
    tm_down = _tile(tm_up, MOE_DOWN_ROW_TILE, BF16_SUBLANES)
    dest, row_src, groups = _group_by_expert(top_idx, n_experts, tm_up)
    xg = _gather_rows(xs, row_src, d)
    act = _moe_swiglu(xg, w_expert_gate_up[0], groups, tm_up)
    y = _moe_down(act, w_expert_down[0], groups, tm_down)
    h = _combine(h, top_w, y, dest)
    return h.reshape(batch, seq, d)
```

```python
import functools
import math

import numpy as np
import jax
import jax.numpy as jnp
from jax import lax
from jax.experimental import pallas as pl
from jax.experimental.pallas import tpu as pltpu

F32 = jnp.float32
BF16 = jnp.bfloat16

MOBA_BLOCK = 256
MOBA_TOP_K = 3
EXPERT_TOP_K = 2
CONV_WIDTH = 3
RMS_EPS = 1e-6
NEG = -1e30

LANES = 128
BF16_SUBLANES = 16
V7X_VMEM_BYTES = 64 * 1024 * 1024
VMEM_LIMIT_BYTES = V7X_VMEM_BYTES - 8 * 1024 * 1024

ROW_TILE = 1024
COL_TILE = 512
MAX_RESIDENT_K = 4096
MOE_ROW_TILE = 512
MOE_SUB_ROWS = 256
NORM_ROW_TILE = 256
GATHER_ROW_TILE = 256
DMA_ISSUE_UNROLL = 8
SLAB_ROWS = 40
CAST_ROWS = 256


def _tile(dim, pref, mult=LANES):
    if dim <= pref:
        return dim
    t = pref - pref % mult
    while t >= mult:
        if dim % t == 0:
            return t
        t -= mult
    return dim


def _cparams(n_axes):
    return pltpu.CompilerParams(dimension_semantics=("arbitrary",) * n_axes,
                                vmem_limit_bytes=VMEM_LIMIT_BYTES)


def _cast_rows(w_ref, wb_ref):
    k = w_ref.shape[0]
    rows = _tile(k, CAST_ROWS, BF16_SUBLANES)

    def step(r, carry):
        sl = pl.ds(pl.multiple_of(r * rows, rows), rows)
        wb_ref[sl, :] = w_ref[sl, :].astype(wb_ref.dtype)
        return carry

    lax.fori_loop(0, k // rows, step, 0)


def _rmsnorm_body(x_ref, g_ref, o_ref):
    x = x_ref[...]
    inv = lax.rsqrt(jnp.mean(x * x, axis=-1, keepdims=True) + RMS_EPS)
    o_ref[...] = (x * inv * g_ref[...]).astype(o_ref.dtype)


def _rmsnorm(x, g):
    n, d = x.shape
    tm = _tile(n, NORM_ROW_TILE, BF16_SUBLANES)
    return pl.pallas_call(
        _rmsnorm_body,
        grid=(n // tm,),
        in_specs=[pl.BlockSpec((tm, d), lambda i: (i, 0)),
                  pl.BlockSpec((1, d), lambda i: (0, 0))],
        out_specs=pl.BlockSpec((tm, d), lambda i: (i, 0)),
        out_shape=jax.ShapeDtypeStruct((n, d), BF16),
        compiler_params=_cparams(1),
        name="rmsnorm",
    )(x, g.reshape(1, d))


def _mm_plain_body(x_ref, w_ref, o_ref, wb_ref):
    @pl.when(pl.program_id(1) == 0)
    def _():
        _cast_rows(w_ref, wb_ref)

    o_ref[...] = jnp.dot(x_ref[...], wb_ref[...], preferred_element_type=F32).astype(o_ref.dtype)


def _mm_residual_body(x_ref, w_ref, r_ref, o_ref, wb_ref):
    @pl.when(pl.program_id(1) == 0)
    def _():
        _cast_rows(w_ref, wb_ref)

    o_ref[...] = r_ref[...] + jnp.dot(x_ref[...], wb_ref[...], preferred_element_type=F32)


def _mm_qknorm_body(x_ref, w_ref, g_ref, o_ref, wb_ref, *, n_norm_tiles, head_dim):
    j = pl.program_id(0)

    @pl.when(pl.program_id(1) == 0)
    def _():
        _cast_rows(w_ref, wb_ref)

    acc = jnp.dot(x_ref[...], wb_ref[...], preferred_element_type=F32)

    @pl.when(j < n_norm_tiles)
    def _():
        for hh in range(acc.shape[1] // head_dim):
            sl = slice(hh * head_dim, (hh + 1) * head_dim)
            a = acc[:, sl]
            inv = lax.rsqrt(jnp.mean(a * a, axis=-1, keepdims=True) + RMS_EPS)
            o_ref[:, sl] = (a * inv * g_ref[:, sl]).astype(o_ref.dtype)

    @pl.when(j >= n_norm_tiles)
    def _():
        o_ref[...] = acc.astype(o_ref.dtype)


def _swiglu_act(g, u):
    return g * (1.0 / (1.0 + jnp.exp(-g))) * u


def _mm_swiglu_body(x_ref, wg_ref, wu_ref, o_ref, wgb_ref, wub_ref):
    @pl.when(pl.program_id(1) == 0)
    def _():
        _cast_rows(wg_ref, wgb_ref)
        _cast_rows(wu_ref, wub_ref)

    x = x_ref[...]
    g = jnp.dot(x, wgb_ref[...], preferred_element_type=F32)
    u = jnp.dot(x, wub_ref[...], preferred_element_type=F32)
    o_ref[...] = _swiglu_act(g, u).astype(o_ref.dtype)


def _mm(x, w, *, out_dtype=BF16, residual=None, qk_gain=None, n_norm_cols=0, head_dim=LANES,
        k_blocks=1, k_block=0):
    m = x.shape[0]
    k = x.shape[1] // k_blocks
    n = w.shape[1]
    tm = _tile(m, ROW_TILE, BF16_SUBLANES)
    tn = _tile(n, COL_TILE)
    grid = (n // tn, m // tm)
    in_specs = [pl.BlockSpec((tm, k), lambda j, i: (i, k_block)),
                pl.BlockSpec((k, tn), lambda j, i: (k_block, j))]
    args = [x, w]
    if residual is not None:
        body = _mm_residual_body
        in_specs.append(pl.BlockSpec((tm, tn), lambda j, i: (i, j)))
        args.append(residual)
        out_dtype = F32
        name = "mm_residual"
    elif qk_gain is not None:
        assert n_norm_cols % tn == 0 and tn % head_dim == 0
        body = functools.partial(_mm_qknorm_body, n_norm_tiles=n_norm_cols // tn, head_dim=head_dim)
        in_specs.append(pl.BlockSpec((1, tn), lambda j, i: (0, j)))
        args.append(qk_gain)
        name = "mm_qknorm"
    else:
        body = _mm_plain_body
        name = "mm_plain"
    return pl.pallas_call(
        body,
        grid=grid,
        in_specs=in_specs,
        out_specs=pl.BlockSpec((tm, tn), lambda j, i: (i, j)),
        out_shape=jax.ShapeDtypeStruct((m, n), out_dtype),
        scratch_shapes=[pltpu.VMEM((k, tn), BF16)],
        compiler_params=_cparams(2),
        name=name,
    )(*args)


def _mm_swiglu(x, w_gate_up):
    m, k = x.shape
    f = w_gate_up.shape[1] // 2
    tm = _tile(m, ROW_TILE, BF16_SUBLANES)
    tn = _tile(f, COL_TILE // 2)
    nj = f // tn
    return pl.pallas_call(
        _mm_swiglu_body,
        grid=(nj, m // tm),
        in_specs=[pl.BlockSpec((tm, k), lambda j, i: (i, 0)),
                  pl.BlockSpec((k, tn), lambda j, i: (0, j)),
                  pl.BlockSpec((k, tn), lambda j, i: (0, nj + j))],
        out_specs=pl.BlockSpec((tm, tn), lambda j, i: (i, j)),
        out_shape=jax.ShapeDtypeStruct((m, f), BF16),
        scratch_shapes=[pltpu.VMEM((k, tn), BF16), pltpu.VMEM((k, tn), BF16)],
        compiler_params=_cparams(2),
        name="mm_swiglu",
    )(x, w_gate_up, w_gate_up)


def _mm_long_k_residual(x, w, residual):
    k = x.shape[1]
    k_blocks = 1
    while k // k_blocks > MAX_RESIDENT_K or k % k_blocks or (k // k_blocks) % LANES:
        k_blocks += 1
    for b in range(k_blocks):
        residual = _mm(x, w, residual=residual, k_blocks=k_blocks, k_block=b)
    return residual


LOG2E = math.log2(math.e)
SLOPE_TERMS = 3


def _col_reduce(x, op):
    rows = x.shape[0]
    while rows > 8 and rows % 16 == 0:
        rows //= 2
        x = op(x[:rows, :], x[rows:, :])
    red = jnp.max if op is jnp.maximum else jnp.sum
    return red(x, axis=0, keepdims=True)


def _attn_body(qc_ref, q_ref, k_ref, v_ref, o_ref, ka_ref, qa_ref, vt_ref, *, n_blocks, top_k):
    blk = MOBA_BLOCK
    blk_shift = int(math.log2(blk))
    s_len = n_blocks * blk
    hd = q_ref.shape[1]
    nt_dims = (((1,), (1,)), ((), ()))

    km = jnp.concatenate(
        [jnp.mean(k_ref[n * blk:(n + 1) * blk, :].astype(F32), axis=0, keepdims=True)
         for n in range(n_blocks)], axis=0)
    km_hi = km.astype(BF16)
    km_lo = (km - km_hi.astype(F32)).astype(BF16)
    km2 = jnp.concatenate([km_hi, km_lo], axis=1)
    q = q_ref[...]
    q2 = jnp.concatenate([q, q], axis=1)
    gate = lax.dot_general(km2, q2, nt_dims, preferred_element_type=F32)

    n_iota = lax.broadcasted_iota(jnp.int32, (n_blocks, s_len), 0)
    q_blk = lax.shift_right_logical(lax.broadcasted_iota(jnp.int32, (n_blocks, s_len), 1), blk_shift)
    valid = n_iota < q_blk
    g = jnp.where(valid, gate, -jnp.inf)
    rank = jnp.zeros((n_blocks, s_len), jnp.int32)
    for mm in range(n_blocks):
        gm = g[mm:mm + 1, :]
        tie = jnp.where(n_iota > mm, 1, 0)
        rank = rank + jnp.where(gm > g, 1, jnp.where(gm == g, tie, 0))
    sel = jnp.where(valid, jnp.where(rank < top_k, 0.0, NEG),
                    jnp.where(n_iota == q_blk, 0.0, NEG))
    sel_t = jnp.concatenate([sel, jnp.zeros((LANES - n_blocks, s_len), F32)], axis=0).T

    qa_ref[:, :hd] = q
    qa_ref[:, hd:] = (sel_t + qc_ref[...]).astype(BF16)
    t = lax.broadcasted_iota(jnp.int32, (s_len, LANES), 0)
    lane = lax.broadcasted_iota(jnp.int32, (s_len, LANES), 1)
    t_blk = lax.shift_right_logical(t, blk_shift)
    t_off = jnp.bitwise_and(t, blk - 1)
    k_cols = jnp.where(lane < n_blocks, jnp.where(lane == t_blk, 1.0, 0.0),
                       jnp.where(lane < n_blocks + SLOPE_TERMS, t_off.astype(F32),
                                 jnp.where(lane < n_blocks + 2 * SLOPE_TERMS, t_blk.astype(F32), 0.0)))
    ka_ref[:, :hd] = k_ref[...]
    ka_ref[:, hd:] = k_cols.astype(BF16)

    vt_ref[...] = v_ref[...].astype(F32).T.astype(vt_ref.dtype)

    r = lax.broadcasted_iota(jnp.int32, (blk, blk), 1)
    c = lax.broadcasted_iota(jnp.int32, (blk, blk), 0)
    causal = c <= r

    def scores(i):
        return lax.dot_general(ka_ref[:(i + 1) * blk, :], qa_ref[i * blk:(i + 1) * blk, :], nt_dims,
                               preferred_element_type=F32)

    ahead = 2
    s_queue = [scores(i) for i in range(min(ahead, n_blocks))]
    for i in range(n_blocks):
        sl = slice(i * blk, (i + 1) * blk)
        n_keys = (i + 1) * blk
        s = s_queue.pop(0)
        if i + ahead < n_blocks:
            s_queue.append(scores(i + ahead))
        s_own = jnp.where(causal, s[i * blk:, :], NEG)
        m = _col_reduce(s_own, jnp.maximum)
        if i > 0:
            s_past = s[:i * blk, :]
            m = jnp.maximum(m, _col_reduce(s_past, jnp.maximum))
        p_own = jnp.exp2(s_own - m)
        l = _col_reduce(p_own, jnp.add)
        p = p_own.astype(BF16)
        if i > 0:
            p_past = jnp.exp2(s_past - m)
            l = l + _col_reduce(p_past, jnp.add)
            p = jnp.concatenate([p_past.astype(BF16), p], axis=0)
        acc = jnp.dot(vt_ref[:, :n_keys], p, preferred_element_type=F32)
        o_ref[sl, :] = (acc * (1.0 / l)).T.astype(o_ref.dtype)


def _moba_attention(qkv, batch, seq, n_heads, head_dim):
    assert seq % MOBA_BLOCK == 0 and head_dim == LANES
    n_blocks = seq // MOBA_BLOCK
    assert n_blocks + 2 * SLOPE_TERMS <= LANES
    top_k = min(MOBA_TOP_K, n_blocks - 1)
    rest = jnp.asarray(np.power(2.0, -8.0 * np.arange(1, n_heads + 1) / n_heads).astype(np.float32)) * LOG2E
    terms = []
    for _ in range(SLOPE_TERMS):
        term = rest.astype(BF16).astype(F32)
        terms.append(term)
        rest = rest - term
    cols = jnp.stack(terms + [MOBA_BLOCK * term for term in terms], axis=1)
    q_cols = jnp.zeros((n_heads, 1, LANES), F32).at[:, 0, n_blocks:n_blocks + 2 * SLOPE_TERMS].set(cols)
    blk_spec = lambda off: pl.BlockSpec((seq, head_dim), lambda b, h: (b, off + h))
    return pl.pallas_call(
        functools.partial(_attn_body, n_blocks=n_blocks, top_k=top_k),
        grid=(batch, n_heads),
        in_specs=[pl.BlockSpec((None, 1, LANES), lambda b, h: (h, 0, 0)),
                  blk_spec(0), blk_spec(n_heads), blk_spec(2 * n_heads)],
        out_specs=pl.BlockSpec((seq, head_dim), lambda b, h: (b, h)),
        out_shape=jax.ShapeDtypeStruct((batch * seq, n_heads * head_dim), BF16),
        scratch_shapes=[pltpu.VMEM((seq, 2 * head_dim), BF16),
                        pltpu.VMEM((seq, 2 * head_dim), BF16),
                        pltpu.VMEM((head_dim, seq), BF16)],
        compiler_params=_cparams(2),
        name="moba_attention",
    )(q_cols, qkv, qkv, qkv)


def _conv_gate_body(b_ref, c_ref, h_ref, cp_ref, hp_ref, w_ref, o_ref, *, seq):
    tm = c_ref.shape[0]
    u = c_ref[...].astype(F32) * h_ref[...].astype(F32)
    keep = jnp.where((pl.program_id(0) * tm) % seq == 0, 0.0, 1.0)
    up = cp_ref[...].astype(F32) * hp_ref[...].astype(F32) * keep
    last = up.shape[0] - 1
    row = lax.broadcasted_iota(jnp.int32, u.shape, 0)
    u1 = jnp.where(row == 0, up[last:last + 1, :], pltpu.roll(u, 1, 0))
    u2 = jnp.where(row == 0, up[last - 1:last, :],
                   jnp.where(row == 1, up[last:last + 1, :], pltpu.roll(u, 2, 0)))
    y = w_ref[0:1, :] * u2 + w_ref[1:2, :] * u1 + w_ref[2:3, :] * u
    o_ref[...] = (b_ref[...].astype(F32) * y).astype(o_ref.dtype)


def _conv_gate(bch, conv_w, seq):
    n, d3 = bch.shape
    d = d3 // 3
    tm = _tile(seq, 512, BF16_SUBLANES)
    tc = _tile(d, 1024)
    nc = d // tc
    halo = BF16_SUBLANES
    hb = tm // halo
    prev = lambda off: (lambda i, j: (jnp.maximum(i * hb - 1, 0), off + j))
    return pl.pallas_call(
        functools.partial(_conv_gate_body, seq=seq),
        grid=(n // tm, nc),
        in_specs=[pl.BlockSpec((tm, tc), lambda i, j: (i, j)),
                  pl.BlockSpec((tm, tc), lambda i, j: (i, nc + j)),
                  pl.BlockSpec((tm, tc), lambda i, j: (i, 2 * nc + j)),
                  pl.BlockSpec((halo, tc), prev(nc)),
                  pl.BlockSpec((halo, tc), prev(2 * nc)),
                  pl.BlockSpec((CONV_WIDTH, tc), lambda i, j: (0, j))],
        out_specs=pl.BlockSpec((tm, tc), lambda i, j: (i, j)),
        out_shape=jax.ShapeDtypeStruct((n, d), BF16),
        compiler_params=_cparams(2),
        name="conv_gate",
    )(bch, bch, bch, bch, bch, conv_w)


def _router_body(x_ref, g_ref, wr_ref, xs_ref, idx_ref, wgt_ref, *, n_experts):
    x = x_ref[...]
    y = x * lax.rsqrt(jnp.mean(x * x, axis=-1, keepdims=True) + RMS_EPS) * g_ref[...]
    tm, d = y.shape
    for s in range(d // LANES):
        xs_ref[pl.ds(s, tm, stride=SLAB_ROWS), :] = y[:, s * LANES:(s + 1) * LANES]
    for s in range(d // LANES, SLAB_ROWS):
        xs_ref[pl.ds(s, tm, stride=SLAB_ROWS), :] = jnp.zeros((tm, LANES), F32)
    logits = jnp.dot(y, wr_ref[...], precision=lax.Precision.HIGHEST, preferred_element_type=F32)
    lane = lax.broadcasted_iota(jnp.int32, logits.shape, 1).astype(F32)
    big = float(logits.shape[1])
    lg = jnp.where(lane < n_experts, logits, -jnp.inf)
    m1 = jnp.max(lg, axis=-1, keepdims=True)
    i1 = jnp.min(jnp.where(lg == m1, lane, big), axis=-1, keepdims=True)
    lg2 = jnp.where(lane == i1, -jnp.inf, lg)
    m2 = jnp.max(lg2, axis=-1, keepdims=True)
    i2 = jnp.min(jnp.where(lg2 == m2, lane, big), axis=-1, keepdims=True)
    e = jnp.exp(m2 - m1)
    w1 = 1.0 / (1.0 + e)
    w2 = e * w1
    k = idx_ref.shape[1]
    idx_ref[...] = jnp.where(lane == 0.0, i1, i2)[:, :k].astype(jnp.int32)
    wgt_ref[...] = jnp.where(lane == 0.0, w1, w2)[:, :k]


def _router(h, g, w_router):
    n, d = h.shape
    assert d % LANES == 0 and d // LANES <= SLAB_ROWS
    n_experts = w_router.shape[1]
    tm = _tile(n, NORM_ROW_TILE, 8)
    wr = jnp.pad(w_router, ((0, 0), (0, LANES - n_experts)))
    return pl.pallas_call(
        functools.partial(_router_body, n_experts=n_experts),
        grid=(n // tm,),
        in_specs=[pl.BlockSpec((tm, d), lambda i: (i, 0)),
                  pl.BlockSpec((1, d), lambda i: (0, 0)),
                  pl.BlockSpec((d, LANES), lambda i: (0, 0))],
        out_specs=[pl.BlockSpec((tm * SLAB_ROWS, LANES), lambda i: (i, 0)),
                   pl.BlockSpec((tm, EXPERT_TOP_K), lambda i: (i, 0)),
                   pl.BlockSpec((tm, EXPERT_TOP_K), lambda i: (i, 0))],
        out_shape=[jax.ShapeDtypeStruct((n * SLAB_ROWS, LANES), F32),
                   jax.ShapeDtypeStruct((n, EXPERT_TOP_K), jnp.int32),
                   jax.ShapeDtypeStruct((n, EXPERT_TOP_K), F32)],
        compiler_params=_cparams(1),
        name="router",
    )(h, g.reshape(1, d), wr)


def _group_by_expert(top_idx, n_experts, align):
    n = top_idx.shape[0]
    n_pairs = n * EXPERT_TOP_K
    e_flat = top_idx.reshape(-1)
    onehot = (e_flat[:, None] == jnp.arange(n_experts, dtype=jnp.int32)[None, :]).astype(jnp.int32)
    csum = jnp.cumsum(onehot, axis=0)
    rank = jnp.sum(onehot * csum, axis=1) - 1
    counts = csum[-1]
    padded = (counts + align - 1) // align * align
    ends = jnp.cumsum(padded)
    starts = ends - padded
    dest = (jnp.sum(onehot * starts[None, :], axis=1) + rank).astype(jnp.int32)
    p_rows = n_pairs + n_experts * align
    row_src = jnp.zeros((p_rows,), jnp.int32).at[dest].set(
        jnp.arange(n_pairs, dtype=jnp.int32) // EXPERT_TOP_K)
    return dest, row_src, (starts, starts + counts, ends)


def _tile_table(groups, p_rows, tm):
    starts, valid_ends, ends = groups
    n_experts = ends.shape[0]
    tile_start = jnp.arange(p_rows // tm, dtype=jnp.int32) * tm
    before = jnp.sum((tile_start[:, None] >= ends[None, :]).astype(jnp.int32), axis=1)
    live = before < n_experts
    n_live = ends[-1] // tm
    expert = jnp.minimum(before, n_experts - 1)
    rows = jnp.where(live, jnp.clip(valid_ends[expert] - tile_start, 0, tm), 0)
    expert = jnp.where(live, expert, expert[jnp.maximum(n_live - 1, 0)])
    return expert.astype(jnp.int32), rows.astype(jnp.int32), n_live.reshape(1).astype(jnp.int32)


def _gather_rows_body(src_ref, x_hbm, o_ref, buf0_ref, buf1_ref, sems):
    tg, d = o_ref.shape
    n_slab = d // LANES
    i = pl.program_id(0)
    bufs = (buf0_ref, buf1_ref)

    def start_tile(tile, slot):
        def start(r, carry):
            src = pl.multiple_of(src_ref[tile * tg + r] * SLAB_ROWS, SLAB_ROWS)
            dst = pl.multiple_of(r * SLAB_ROWS, SLAB_ROWS)
            pltpu.make_async_copy(x_hbm.at[pl.ds(src, n_slab), :],
                                  bufs[slot].at[pl.ds(dst, n_slab), :], sems.at[slot]).start()
            return carry

        lax.fori_loop(0, tg, start, 0, unroll=DMA_ISSUE_UNROLL)

    def finish_tile(slot):
        pltpu.make_async_copy(x_hbm.at[pl.ds(0, tg * n_slab), :],
                              bufs[slot].at[pl.ds(0, tg * n_slab), :], sems.at[slot]).wait()
        for s in range(n_slab):
            o_ref[:, s * LANES:(s + 1) * LANES] = (
                bufs[slot][pl.ds(s, tg, stride=SLAB_ROWS), :].astype(o_ref.dtype))

    @pl.when(i == 0)
    def _():
        start_tile(0, 0)

    for slot in (0, 1):
        @pl.when(jnp.logical_and(i % 2 == slot, i + 1 < pl.num_programs(0)))
        def _(slot=slot):
            start_tile(i + 1, 1 - slot)

        @pl.when(i % 2 == slot)
        def _(slot=slot):
            finish_tile(slot)


def _gather_rows(xs, row_src, d):
    p_rows = row_src.shape[0]
    tg = _tile(p_rows, GATHER_ROW_TILE, BF16_SUBLANES)
    return pl.pallas_call(
        _gather_rows_body,
        grid_spec=pltpu.PrefetchScalarGridSpec(
            num_scalar_prefetch=1,
            grid=(p_rows // tg,),
            in_specs=[pl.BlockSpec(memory_space=pl.ANY)],
            out_specs=pl.BlockSpec((tg, d), lambda i, src: (i, 0)),
            scratch_shapes=[pltpu.VMEM((tg * SLAB_ROWS, LANES), F32),
                            pltpu.VMEM((tg * SLAB_ROWS, LANES), F32),
                            pltpu.SemaphoreType.DMA((2,))]),
        out_shape=jax.ShapeDtypeStruct((p_rows, d), BF16),
        compiler_params=_cparams(1),
        name="moe_gather",
    )(row_src, xs)


def _first_tile_of_expert(te_ref, i):
    return jnp.logical_or(i == 0, te_ref[i] != te_ref[jnp.maximum(i - 1, 0)])


def _for_valid_rows(rows, o_ref, compute):
    tm = o_ref.shape[0]
    sub = min(MOE_SUB_ROWS, tm)
    for m in range(sub, tm + 1, sub):
        @pl.when(jnp.logical_and(rows > m - sub, rows <= m))
        def _(m=m):
            o_ref[:m, :] = compute(m)
            if m < tm:
                o_ref[m:, :] = jnp.zeros((tm - m, o_ref.shape[1]), o_ref.dtype)

    @pl.when(rows <= 0)
    def _():
        o_ref[...] = jnp.zeros(o_ref.shape, o_ref.dtype)


def _moe_swiglu_body(te_ref, tr_ref, nl_ref, x_ref, wg_ref, wu_ref, o_ref, wgb_ref, wub_ref):
    i = pl.program_id(1)

    @pl.when(_first_tile_of_expert(te_ref, i))
    def _():
        _cast_rows(wg_ref, wgb_ref)
        _cast_rows(wu_ref, wub_ref)

    def compute(m):
        x = x_ref[:m, :]
        g = jnp.dot(x, wgb_ref[...], preferred_element_type=F32)
        u = jnp.dot(x, wub_ref[...], preferred_element_type=F32)
        return _swiglu_act(g, u).astype(o_ref.dtype)

    _for_valid_rows(tr_ref[i], o_ref, compute)


def _moe_down_body(te_ref, tr_ref, nl_ref, x_ref, w_ref, o_ref, wb_ref):
    i = pl.program_id(1)

    @pl.when(_first_tile_of_expert(te_ref, i))
    def _():
        _cast_rows(w_ref, wb_ref)

    def compute(m):
        return jnp.dot(x_ref[:m, :], wb_ref[...], preferred_element_type=F32)

    _for_valid_rows(tr_ref[i], o_ref, compute)


def _live_tile(i, nl):
    return jnp.minimum(i, nl[0] - 1)


def _moe_swiglu(xg, w_gate_up, groups, tm):
    p_rows, k = xg.shape
    f = w_gate_up.shape[2] // 2
    tn = _tile(f, COL_TILE)
    nj = f // tn
    tile_expert, tile_rows, n_live = _tile_table(groups, p_rows, tm)
    return pl.pallas_call(
        _moe_swiglu_body,
        grid_spec=pltpu.PrefetchScalarGridSpec(
            num_scalar_prefetch=3,
            grid=(nj, p_rows // tm),
            in_specs=[pl.BlockSpec((tm, k), lambda j, i, te, tr, nl: (_live_tile(i, nl), 0)),
                      pl.BlockSpec((None, k, tn), lambda j, i, te, tr, nl: (te[i], 0, j)),
                      pl.BlockSpec((None, k, tn), lambda j, i, te, tr, nl: (te[i], 0, nj + j))],
            out_specs=pl.BlockSpec((tm, tn), lambda j, i, te, tr, nl: (i, j)),
            scratch_shapes=[pltpu.VMEM((k, tn), BF16), pltpu.VMEM((k, tn), BF16)]),
        out_shape=jax.ShapeDtypeStruct((p_rows, f), BF16),
        compiler_params=_cparams(2),
        name="moe_swiglu",
    )(tile_expert, tile_rows, n_live, xg, w_gate_up, w_gate_up)


def _moe_down(act, w_down, groups, tm):
    p_rows, k = act.shape
    n = w_down.shape[2]
    tn = _tile(n, COL_TILE)
    tile_expert, tile_rows, n_live = _tile_table(groups, p_rows, tm)
    return pl.pallas_call(
        _moe_down_body,
        grid_spec=pltpu.PrefetchScalarGridSpec(
            num_scalar_prefetch=3,
            grid=(n // tn, p_rows // tm),
            in_specs=[pl.BlockSpec((tm, k), lambda j, i, te, tr, nl: (_live_tile(i, nl), 0)),
                      pl.BlockSpec((None, k, tn), lambda j, i, te, tr, nl: (te[i], 0, j))],
            out_specs=pl.BlockSpec((tm, tn), lambda j, i, te, tr, nl: (i, j)),
            scratch_shapes=[pltpu.VMEM((k, tn), BF16)]),
        out_shape=jax.ShapeDtypeStruct((p_rows, n), F32),
        compiler_params=_cparams(2),
        name="moe_down",
    )(tile_expert, tile_rows, n_live, act, w_down)


def _combine_body(dest_ref, h_ref, w_ref, y_hbm, o_ref, ya0_ref, yb0_ref, ya1_ref, yb1_ref, sems):
    tc = h_ref.shape[0]
    i = pl.program_id(0)
    bufs = ((ya0_ref, yb0_ref), (ya1_ref, yb1_ref))

    def start_tile(tile, slot):
        def start(r, carry):
            pair = (tile * tc + r) * EXPERT_TOP_K
            for c in range(EXPERT_TOP_K):
                pltpu.make_async_copy(y_hbm.at[pl.ds(dest_ref[pair + c], 1), :],
                                      bufs[slot][c].at[pl.ds(r, 1), :], sems.at[slot, c]).start()
            return carry

        lax.fori_loop(0, tc, start, 0, unroll=DMA_ISSUE_UNROLL)

    def finish_tile(slot):
        for c in range(EXPERT_TOP_K):
            pltpu.make_async_copy(y_hbm.at[pl.ds(0, tc), :], bufs[slot][c], sems.at[slot, c]).wait()
        w = w_ref[...]
        o_ref[...] = h_ref[...] + (w[:, 0:1] * bufs[slot][0][...] + w[:, 1:2] * bufs[slot][1][...])

    @pl.when(i == 0)
    def _():
        start_tile(0, 0)

    for slot in (0, 1):
        @pl.when(jnp.logical_and(i % 2 == slot, i + 1 < pl.num_programs(0)))
        def _(slot=slot):
            start_tile(i + 1, 1 - slot)

        @pl.when(i % 2 == slot)
        def _(slot=slot):
            finish_tile(slot)


def _combine(h, top_w, y, dest):
    n, d = h.shape
    tc = _tile(n, GATHER_ROW_TILE, 8)
    return pl.pallas_call(
        _combine_body,
        grid_spec=pltpu.PrefetchScalarGridSpec(
            num_scalar_prefetch=1,
            grid=(n // tc,),
            in_specs=[pl.BlockSpec((tc, d), lambda i, dst: (i, 0)),
                      pl.BlockSpec((tc, EXPERT_TOP_K), lambda i, dst: (i, 0)),
                      pl.BlockSpec(memory_space=pl.ANY)],
            out_specs=pl.BlockSpec((tc, d), lambda i, dst: (i, 0)),
            scratch_shapes=[pltpu.VMEM((tc, d), F32) for _ in range(2 * EXPERT_TOP_K)]
            + [pltpu.SemaphoreType.DMA((2, EXPERT_TOP_K))]),
        out_shape=jax.ShapeDtypeStruct((n, d), F32),
        compiler_params=_cparams(1),
        name="moe_combine",
    )(dest, h, top_w, y)


def kernel(x, attn_norm, w_qkv, q_norm, k_norm, w_attn_out, ffn_norm, w_ffn_gate_up, w_ffn_down,
           conv_norm, w_conv_in, conv_w, w_conv_out, moe_norm, w_router, w_expert_gate_up,
           w_expert_down):
    batch, seq, d = x.shape
    head_dim = q_norm.shape[-1]
    n_heads = d // head_dim
    n_experts = w_router.shape[-1]
    assert attn_norm.shape[0] == 1 and conv_norm.shape[0] == 1, "two-layer trunk: attention, then conv"
    n_tok = batch * seq
    h = x.reshape(n_tok, d)

    gain = jnp.concatenate([jnp.tile(q_norm[0] * (LOG2E / math.sqrt(head_dim)), n_heads),
                            jnp.tile(k_norm[0], n_heads),
                            jnp.ones((d,), F32)]).reshape(1, 3 * d)
    xn = _rmsnorm(h, attn_norm[0])
    qkv = _mm(xn, w_qkv[0], qk_gain=gain, n_norm_cols=2 * d, head_dim=head_dim)
    o = _moba_attention(qkv, batch, seq, n_heads, head_dim)
    h = _mm(o, w_attn_out[0], residual=h)
    xn = _rmsnorm(h, ffn_norm[0])
    act = _mm_swiglu(xn, w_ffn_gate_up[0])
    h = _mm_long_k_residual(act, w_ffn_down[0], h)

    xn = _rmsnorm(h, conv_norm[0])
    bch = _mm(xn, w_conv_in[0])
    z = _conv_gate(bch, conv_w[0], seq)
    h = _mm(z, w_conv_out[0], residual=h)
    xs, top_idx, top_w = _router(h, moe_norm[0], w_router[0])
    tm = _tile(n_tok, MOE_ROW_TILE, BF16_SUBLANES)
    dest, row_src, groups = _group_by_expert(top_idx, n_experts, tm)
    xg = _gather_rows(xs, row_src, d)
    act = _moe_swiglu(xg, w_expert_gate_up[0], groups, tm)
    y = _moe_down(act, w_expert_down[0], groups, tm)
    h = _combine(h, top_w, y, dest)
    return h.reshape(batch, seq, d)
```

```python
import functools
import math

import numpy as np
import jax
import jax.numpy as jnp
from jax import lax
from jax.experimental import pallas as pl
from jax.experimental.pallas import tpu as pltpu

F32 = jnp.float32
BF16 = jnp.bfloat16

MOBA_BLOCK = 256
MOBA_TOP_K = 3
EXPERT_TOP_K = 2
CONV_WIDTH = 3
RMS_EPS = 1e-6
NEG = -1e30

LANES = 128
BF16_SUBLANES = 16
V7X_VMEM_BYTES = 64 * 1024 * 1024
VMEM_LIMIT_BYTES = V7X_VMEM_BYTES - 8 * 1024 * 1024

ROW_TILE = 1024
COL_TILE = 512
MAX_RESIDENT_K = 4096
MOE_ROW_TILE = 512
MOE_SUB_ROWS = 256
NORM_ROW_TILE = 256
GATHER_ROW_TILE = 256
DMA_ISSUE_UNROLL = 8
SLAB_ROWS = 40
FRESH_K_SLAB = 512


def _tile(dim, pref, mult=LANES):
    if dim <= pref:
        return dim
    t = pref - pref % mult
    while t >= mult:
        if dim % t == 0:
            return t
        t -= mult
    return dim


def _cparams(n_axes):
    return pltpu.CompilerParams(dimension_semantics=("arbitrary",) * n_axes,
                                vmem_limit_bytes=VMEM_LIMIT_BYTES)


def _dot_fresh(x, w_ref, wb_ref):
    k = w_ref.shape[0]
    kc = _tile(k, FRESH_K_SLAB, BF16_SUBLANES)
    acc = None
    for c in range(k // kc):
        sl = slice(c * kc, (c + 1) * kc)
        wb_ref[sl, :] = w_ref[sl, :].astype(wb_ref.dtype)
        part = jnp.dot(x[:, sl], wb_ref[sl, :], preferred_element_type=F32)
        acc = part if acc is None else acc + part
    return acc


def _dot_converted(x, w_ref, wb_ref):
    return jnp.dot(x, wb_ref[...], preferred_element_type=F32)


def _with_weights(fresh, emit):
    @pl.when(fresh)
    def _():
        emit(_dot_fresh)

    @pl.when(jnp.logical_not(fresh))
    def _():
        emit(_dot_converted)


def _rmsnorm_body(x_ref, g_ref, o_ref):
    x = x_ref[...]
    inv = lax.rsqrt(jnp.mean(x * x, axis=-1, keepdims=True) + RMS_EPS)
    o_ref[...] = (x * inv * g_ref[...]).astype(o_ref.dtype)


def _rmsnorm(x, g):
    n, d = x.shape
    tm = _tile(n, NORM_ROW_TILE, BF16_SUBLANES)
    return pl.pallas_call(
        _rmsnorm_body,
        grid=(n // tm,),
        in_specs=[pl.BlockSpec((tm, d), lambda i: (i, 0)),
                  pl.BlockSpec((1, d), lambda i: (0, 0))],
        out_specs=pl.BlockSpec((tm, d), lambda i: (i, 0)),
        out_shape=jax.ShapeDtypeStruct((n, d), BF16),
        compiler_params=_cparams(1),
        name="rmsnorm",
    )(x, g.reshape(1, d))


def _mm_plain_body(x_ref, w_ref, o_ref, wb_ref):
    def emit(dot):
        o_ref[...] = dot(x_ref[...], w_ref, wb_ref).astype(o_ref.dtype)

    _with_weights(pl.program_id(1) == 0, emit)


def _mm_residual_body(x_ref, w_ref, r_ref, o_ref, wb_ref):
    def emit(dot):
        o_ref[...] = r_ref[...] + dot(x_ref[...], w_ref, wb_ref)

    _with_weights(pl.program_id(1) == 0, emit)


def _mm_qknorm_body(x_ref, w_ref, g_ref, o_ref, wb_ref, *, n_norm_tiles, head_dim):
    j = pl.program_id(0)

    def emit(dot):
        acc = dot(x_ref[...], w_ref, wb_ref)

        @pl.when(j < n_norm_tiles)
        def _():
            for hh in range(acc.shape[1] // head_dim):
                sl = slice(hh * head_dim, (hh + 1) * head_dim)
                a = acc[:, sl]
                inv = lax.rsqrt(jnp.mean(a * a, axis=-1, keepdims=True) + RMS_EPS)
                o_ref[:, sl] = (a * inv * g_ref[:, sl]).astype(o_ref.dtype)

        @pl.when(j >= n_norm_tiles)
        def _():
            o_ref[...] = acc.astype(o_ref.dtype)

    _with_weights(pl.program_id(1) == 0, emit)


def _swiglu_act(g, u):
    return g * (1.0 / (1.0 + jnp.exp(-g))) * u


def _mm_swiglu_body(x_ref, wg_ref, wu_ref, o_ref, wgb_ref, wub_ref):
    def emit(dot):
        x = x_ref[...]
        g = dot(x, wg_ref, wgb_ref)
        u = dot(x, wu_ref, wub_ref)
        o_ref[...] = _swiglu_act(g, u).astype(o_ref.dtype)

    _with_weights(pl.program_id(1) == 0, emit)


def _mm(x, w, *, out_dtype=BF16, residual=None, qk_gain=None, n_norm_cols=0, head_dim=LANES,
        k_blocks=1, k_block=0):
    m = x.shape[0]
    k = x.shape[1] // k_blocks
    n = w.shape[1]
    tm = _tile(m, ROW_TILE, BF16_SUBLANES)
    tn = _tile(n, COL_TILE)
    grid = (n // tn, m // tm)
    in_specs = [pl.BlockSpec((tm, k), lambda j, i: (i, k_block)),
                pl.BlockSpec((k, tn), lambda j, i: (k_block, j))]
    args = [x, w]
    if residual is not None:
        body = _mm_residual_body
        in_specs.append(pl.BlockSpec((tm, tn), lambda j, i: (i, j)))
        args.append(residual)
        out_dtype = F32
        name = "mm_residual"
    elif qk_gain is not None:
        assert n_norm_cols % tn == 0 and tn % head_dim == 0
        body = functools.partial(_mm_qknorm_body, n_norm_tiles=n_norm_cols // tn, head_dim=head_dim)
        in_specs.append(pl.BlockSpec((1, tn), lambda j, i: (0, j)))
        args.append(qk_gain)
        name = "mm_qknorm"
    else:
        body = _mm_plain_body
        name = "mm_plain"
    return pl.pallas_call(
        body,
        grid=grid,
        in_specs=in_specs,
        out_specs=pl.BlockSpec((tm, tn), lambda j, i: (i, j)),
        out_shape=jax.ShapeDtypeStruct((m, n), out_dtype),
        scratch_shapes=[pltpu.VMEM((k, tn), BF16)],
        compiler_params=_cparams(2),
        name=name,
    )(*args)


def _mm_swiglu(x, w_gate_up):
    m, k = x.shape
    f = w_gate_up.shape[1] // 2
    tm = _tile(m, ROW_TILE, BF16_SUBLANES)
    tn = _tile(f, COL_TILE // 2)
    nj = f // tn
    return pl.pallas_call(
        _mm_swiglu_body,
        grid=(nj, m // tm),
        in_specs=[pl.BlockSpec((tm, k), lambda j, i: (i, 0)),
                  pl.BlockSpec((k, tn), lambda j, i: (0, j)),
                  pl.BlockSpec((k, tn), lambda j, i: (0, nj + j))],
        out_specs=pl.BlockSpec((tm, tn), lambda j, i: (i, j)),
        out_shape=jax.ShapeDtypeStruct((m, f), BF16),
        scratch_shapes=[pltpu.VMEM((k, tn), BF16), pltpu.VMEM((k, tn), BF16)],
        compiler_params=_cparams(2),
        name="mm_swiglu",
    )(x, w_gate_up, w_gate_up)


def _mm_long_k_residual(x, w, residual):
    k = x.shape[1]
    k_blocks = 1
    while k // k_blocks > MAX_RESIDENT_K or k % k_blocks or (k // k_blocks) % LANES:
        k_blocks += 1
    for b in range(k_blocks):
        residual = _mm(x, w, residual=residual, k_blocks=k_blocks, k_block=b)
    return residual


LOG2E = math.log2(math.e)
SLOPE_TERMS = 3


def _col_reduce(x, op):
    rows = x.shape[0]
    while rows > 8 and rows % 16 == 0:
        rows //= 2
        x = op(x[:rows, :], x[rows:, :])
    red = jnp.max if op is jnp.maximum else jnp.sum
    return red(x, axis=0, keepdims=True)


def _attn_body(qc_ref, q_ref, k_ref, v_ref, o_ref, ka_ref, qa_ref, vt_ref, *, n_blocks, top_k):
    blk = MOBA_BLOCK
    blk_shift = int(math.log2(blk))
    s_len = n_blocks * blk
    hd = q_ref.shape[1]
    nt_dims = (((1,), (1,)), ((), ()))

    km = jnp.concatenate(
        [jnp.mean(k_ref[n * blk:(n + 1) * blk, :].astype(F32), axis=0, keepdims=True)
         for n in range(n_blocks)], axis=0)
    km_hi = km.astype(BF16)
    km_lo = (km - km_hi.astype(F32)).astype(BF16)
    km2 = jnp.concatenate([km_hi, km_lo], axis=1)
    q = q_ref[...]
    q2 = jnp.concatenate([q, q], axis=1)
    gate = lax.dot_general(km2, q2, nt_dims, preferred_element_type=F32)

    n_iota = lax.broadcasted_iota(jnp.int32, (n_blocks, s_len), 0)
    q_blk = lax.shift_right_logical(lax.broadcasted_iota(jnp.int32, (n_blocks, s_len), 1), blk_shift)
    valid = n_iota < q_blk
    g = jnp.where(valid, gate, -jnp.inf)
    rank = jnp.zeros((n_blocks, s_len), jnp.int32)
    for mm in range(n_blocks):
        gm = g[mm:mm + 1, :]
        tie = jnp.where(n_iota > mm, 1, 0)
        rank = rank + jnp.where(gm > g, 1, jnp.where(gm == g, tie, 0))
    sel = jnp.where(valid, jnp.where(rank < top_k, 0.0, NEG),
                    jnp.where(n_iota == q_blk, 0.0, NEG))
    sel_t = jnp.concatenate([sel, jnp.zeros((LANES - n_blocks, s_len), F32)], axis=0).T

    qa_ref[:, :hd] = q
    qa_ref[:, hd:] = (sel_t + qc_ref[...]).astype(BF16)
    t = lax.broadcasted_iota(jnp.int32, (s_len, LANES), 0)
    lane = lax.broadcasted_iota(jnp.int32, (s_len, LANES), 1)
    t_blk = lax.shift_right_logical(t, blk_shift)
    t_off = jnp.bitwise_and(t, blk - 1)
    k_cols = jnp.where(lane < n_blocks, jnp.where(lane == t_blk, 1.0, 0.0),
                       jnp.where(lane < n_blocks + SLOPE_TERMS, t_off.astype(F32),
                                 jnp.where(lane < n_blocks + 2 * SLOPE_TERMS, t_blk.astype(F32), 0.0)))
    ka_ref[:, :hd] = k_ref[...]
    ka_ref[:, hd:] = k_cols.astype(BF16)

    vt_ref[...] = v_ref[...].astype(F32).T.astype(vt_ref.dtype)

    r = lax.broadcasted_iota(jnp.int32, (blk, blk), 1)
    c = lax.broadcasted_iota(jnp.int32, (blk, blk), 0)
    causal = c <= r

    def scores(i):
        return lax.dot_general(ka_ref[:(i + 1) * blk, :], qa_ref[i * blk:(i + 1) * blk, :], nt_dims,
                               preferred_element_type=F32)

    ahead = 2
    s_queue = [scores(i) for i in range(min(ahead, n_blocks))]
    for i in range(n_blocks):
        sl = slice(i * blk, (i + 1) * blk)
        n_keys = (i + 1) * blk
        s = s_queue.pop(0)
        if i + ahead < n_blocks:
            s_queue.append(scores(i + ahead))
        s_own = jnp.where(causal, s[i * blk:, :], NEG)
        m = _col_reduce(s_own, jnp.maximum)
        if i > 0:
            s_past = s[:i * blk, :]
            m = jnp.maximum(m, _col_reduce(s_past, jnp.maximum))
        p_own = jnp.exp2(s_own - m)
        l = _col_reduce(p_own, jnp.add)
        p = p_own.astype(BF16)
        if i > 0:
            p_past = jnp.exp2(s_past - m)
            l = l + _col_reduce(p_past, jnp.add)
            p = jnp.concatenate([p_past.astype(BF16), p], axis=0)
        acc = jnp.dot(vt_ref[:, :n_keys], p, preferred_element_type=F32)
        o_ref[sl, :] = (acc * (1.0 / l)).T.astype(o_ref.dtype)


def _moba_attention(qkv, batch, seq, n_heads, head_dim):
    assert seq % MOBA_BLOCK == 0 and head_dim == LANES
    n_blocks = seq // MOBA_BLOCK
    assert n_blocks + 2 * SLOPE_TERMS <= LANES
    top_k = min(MOBA_TOP_K, n_blocks - 1)
    rest = jnp.asarray(np.power(2.0, -8.0 * np.arange(1, n_heads + 1) / n_heads).astype(np.float32)) * LOG2E
    terms = []
    for _ in range(SLOPE_TERMS):
        term = rest.astype(BF16).astype(F32)
        terms.append(term)
        rest = rest - term
    cols = jnp.stack(terms + [MOBA_BLOCK * term for term in terms], axis=1)
    q_cols = jnp.zeros((n_heads, 1, LANES), F32).at[:, 0, n_blocks:n_blocks + 2 * SLOPE_TERMS].set(cols)
    blk_spec = lambda off: pl.BlockSpec((seq, head_dim), lambda b, h: (b, off + h))
    return pl.pallas_call(
        functools.partial(_attn_body, n_blocks=n_blocks, top_k=top_k),
        grid=(batch, n_heads),
        in_specs=[pl.BlockSpec((None, 1, LANES), lambda b, h: (h, 0, 0)),
                  blk_spec(0), blk_spec(n_heads), blk_spec(2 * n_heads)],
        out_specs=pl.BlockSpec((seq, head_dim), lambda b, h: (b, h)),
        out_shape=jax.ShapeDtypeStruct((batch * seq, n_heads * head_dim), BF16),
        scratch_shapes=[pltpu.VMEM((seq, 2 * head_dim), BF16),
                        pltpu.VMEM((seq, 2 * head_dim), BF16),
                        pltpu.VMEM((head_dim, seq), BF16)],
        compiler_params=_cparams(2),
        name="moba_attention",
    )(q_cols, qkv, qkv, qkv)


def _conv_gate_body(b_ref, c_ref, h_ref, cp_ref, hp_ref, w_ref, o_ref, *, seq):
    tm = c_ref.shape[0]
    u = c_ref[...].astype(F32) * h_ref[...].astype(F32)
    keep = jnp.where((pl.program_id(0) * tm) % seq == 0, 0.0, 1.0)
    up = cp_ref[...].astype(F32) * hp_ref[...].astype(F32) * keep
    last = up.shape[0] - 1
    row = lax.broadcasted_iota(jnp.int32, u.shape, 0)
    u1 = jnp.where(row == 0, up[last:last + 1, :], pltpu.roll(u, 1, 0))
    u2 = jnp.where(row == 0, up[last - 1:last, :],
                   jnp.where(row == 1, up[last:last + 1, :], pltpu.roll(u, 2, 0)))
    y = w_ref[0:1, :] * u2 + w_ref[1:2, :] * u1 + w_ref[2:3, :] * u
    o_ref[...] = (b_ref[...].astype(F32) * y).astype(o_ref.dtype)


def _conv_gate(bch, conv_w, seq):
    n, d3 = bch.shape
    d = d3 // 3
    tm = _tile(seq, 512, BF16_SUBLANES)
    tc = _tile(d, 1024)
    nc = d // tc
    halo = BF16_SUBLANES
    hb = tm // halo
    prev = lambda off: (lambda i, j: (jnp.maximum(i * hb - 1, 0), off + j))
    return pl.pallas_call(
        functools.partial(_conv_gate_body, seq=seq),
        grid=(n // tm, nc),
        in_specs=[pl.BlockSpec((tm, tc), lambda i, j: (i, j)),
                  pl.BlockSpec((tm, tc), lambda i, j: (i, nc + j)),
                  pl.BlockSpec((tm, tc), lambda i, j: (i, 2 * nc + j)),
                  pl.BlockSpec((halo, tc), prev(nc)),
                  pl.BlockSpec((halo, tc), prev(2 * nc)),
                  pl.BlockSpec((CONV_WIDTH, tc), lambda i, j: (0, j))],
        out_specs=pl.BlockSpec((tm, tc), lambda i, j: (i, j)),
        out_shape=jax.ShapeDtypeStruct((n, d), BF16),
        compiler_params=_cparams(2),
        name="conv_gate",
    )(bch, bch, bch, bch, bch, conv_w)


def _router_body(x_ref, g_ref, wr_ref, xs_ref, idx_ref, wgt_ref, *, n_experts):
    x = x_ref[...]
    y = x * lax.rsqrt(jnp.mean(x * x, axis=-1, keepdims=True) + RMS_EPS) * g_ref[...]
    tm, d = y.shape
    for s in range(d // LANES):
        xs_ref[pl.ds(s, tm, stride=SLAB_ROWS), :] = y[:, s * LANES:(s + 1) * LANES]
    for s in range(d // LANES, SLAB_ROWS):
        xs_ref[pl.ds(s, tm, stride=SLAB_ROWS), :] = jnp.zeros((tm, LANES), F32)
    y_hi = y.astype(BF16)
    y_lo = (y - y_hi.astype(F32)).astype(BF16)
    wr = wr_ref[...]
    wr_hi = wr.astype(BF16)
    wr_lo = (wr - wr_hi.astype(F32)).astype(BF16)
    logits = (jnp.dot(y_hi, wr_hi, preferred_element_type=F32)
              + (jnp.dot(y_hi, wr_lo, preferred_element_type=F32)
                 + jnp.dot(y_lo, wr_hi, preferred_element_type=F32)))
    lane = lax.broadcasted_iota(jnp.int32, logits.shape, 1).astype(F32)
    big = float(logits.shape[1])
    lg = jnp.where(lane < n_experts, logits, -jnp.inf)
    m1 = jnp.max(lg, axis=-1, keepdims=True)
    i1 = jnp.min(jnp.where(lg == m1, lane, big), axis=-1, keepdims=True)
    lg2 = jnp.where(lane == i1, -jnp.inf, lg)
    m2 = jnp.max(lg2, axis=-1, keepdims=True)
    i2 = jnp.min(jnp.where(lg2 == m2, lane, big), axis=-1, keepdims=True)
    e = jnp.exp(m2 - m1)
    w1 = 1.0 / (1.0 + e)
    w2 = e * w1
    k = idx_ref.shape[1]
    idx_ref[...] = jnp.where(lane == 0.0, i1, i2)[:, :k].astype(jnp.int32)
    wgt_ref[...] = jnp.where(lane == 0.0, w1, w2)[:, :k]


def _router(h, g, w_router):
    n, d = h.shape
    assert d % LANES == 0 and d // LANES <= SLAB_ROWS
    n_experts = w_router.shape[1]
    tm = _tile(n, NORM_ROW_TILE, 8)
    wr = jnp.pad(w_router, ((0, 0), (0, LANES - n_experts)))
    return pl.pallas_call(
        functools.partial(_router_body, n_experts=n_experts),
        grid=(n // tm,),
        in_specs=[pl.BlockSpec((tm, d), lambda i: (i, 0)),
                  pl.BlockSpec((1, d), lambda i: (0, 0)),
                  pl.BlockSpec((d, LANES), lambda i: (0, 0))],
        out_specs=[pl.BlockSpec((tm * SLAB_ROWS, LANES), lambda i: (i, 0)),
                   pl.BlockSpec((tm, EXPERT_TOP_K), lambda i: (i, 0)),
                   pl.BlockSpec((tm, EXPERT_TOP_K), lambda i: (i, 0))],
        out_shape=[jax.ShapeDtypeStruct((n * SLAB_ROWS, LANES), F32),
                   jax.ShapeDtypeStruct((n, EXPERT_TOP_K), jnp.int32),
                   jax.ShapeDtypeStruct((n, EXPERT_TOP_K), F32)],
        compiler_params=_cparams(1),
        name="router",
    )(h, g.reshape(1, d), wr)


def _group_by_expert(top_idx, n_experts, align):
    n = top_idx.shape[0]
    n_pairs = n * EXPERT_TOP_K
    e_flat = top_idx.reshape(-1)
    onehot = (e_flat[:, None] == jnp.arange(n_experts, dtype=jnp.int32)[None, :]).astype(jnp.int32)
    csum = jnp.cumsum(onehot, axis=0)
    rank = jnp.sum(onehot * csum, axis=1) - 1
    counts = csum[-1]
    padded = (counts + align - 1) // align * align
    ends = jnp.cumsum(padded)
    starts = ends - padded
    dest = (jnp.sum(onehot * starts[None, :], axis=1) + rank).astype(jnp.int32)
    p_rows = n_pairs + n_experts * align
    row_src = jnp.zeros((p_rows,), jnp.int32).at[dest].set(
        jnp.arange(n_pairs, dtype=jnp.int32) // EXPERT_TOP_K)
    return dest, row_src, (starts, starts + counts, ends)


def _tile_table(groups, p_rows, tm):
    starts, valid_ends, ends = groups
    n_experts = ends.shape[0]
    tile_start = jnp.arange(p_rows // tm, dtype=jnp.int32) * tm
    before = jnp.sum((tile_start[:, None] >= ends[None, :]).astype(jnp.int32), axis=1)
    live = before < n_experts
    n_live = ends[-1] // tm
    expert = jnp.minimum(before, n_experts - 1)
    rows = jnp.where(live, jnp.clip(valid_ends[expert] - tile_start, 0, tm), 0)
    expert = jnp.where(live, expert, expert[jnp.maximum(n_live - 1, 0)])
    return expert.astype(jnp.int32), rows.astype(jnp.int32), n_live.reshape(1).astype(jnp.int32)


def _gather_rows_body(src_ref, x_hbm, o_ref, buf0_ref, buf1_ref, sems):
    tg, d = o_ref.shape
    n_slab = d // LANES
    i = pl.program_id(0)
    bufs = (buf0_ref, buf1_ref)

    def start_tile(tile, slot):
        def start(g, carry):
            for u in range(DMA_ISSUE_UNROLL):
                r = g * DMA_ISSUE_UNROLL + u
                src = pl.multiple_of(src_ref[tile * tg + r] * SLAB_ROWS, SLAB_ROWS)
                dst = pl.multiple_of(r * SLAB_ROWS, SLAB_ROWS)
                pltpu.make_async_copy(x_hbm.at[pl.ds(src, n_slab), :],
                                      bufs[slot].at[pl.ds(dst, n_slab), :],
                                      sems.at[slot]).start(priority=u % 2)
            return carry

        lax.fori_loop(0, tg // DMA_ISSUE_UNROLL, start, 0)

    def finish_tile(slot):
        pltpu.make_async_copy(x_hbm.at[pl.ds(0, tg * n_slab), :],
                              bufs[slot].at[pl.ds(0, tg * n_slab), :], sems.at[slot]).wait()
        for s in range(n_slab):
            o_ref[:, s * LANES:(s + 1) * LANES] = (
                bufs[slot][pl.ds(s, tg, stride=SLAB_ROWS), :].astype(o_ref.dtype))

    @pl.when(i == 0)
    def _():
        start_tile(0, 0)

    for slot in (0, 1):
        @pl.when(jnp.logical_and(i % 2 == slot, i + 1 < pl.num_programs(0)))
        def _(slot=slot):
            start_tile(i + 1, 1 - slot)

        @pl.when(i % 2 == slot)
        def _(slot=slot):
            finish_tile(slot)


def _gather_rows(xs, row_src, d):
    p_rows = row_src.shape[0]
    tg = _tile(p_rows, GATHER_ROW_TILE, BF16_SUBLANES)
    return pl.pallas_call(
        _gather_rows_body,
        grid_spec=pltpu.PrefetchScalarGridSpec(
            num_scalar_prefetch=1,
            grid=(p_rows // tg,),
            in_specs=[pl.BlockSpec(memory_space=pl.ANY)],
            out_specs=pl.BlockSpec((tg, d), lambda i, src: (i, 0)),
            scratch_shapes=[pltpu.VMEM((tg * SLAB_ROWS, LANES), F32),
                            pltpu.VMEM((tg * SLAB_ROWS, LANES), F32),
                            pltpu.SemaphoreType.DMA((2,))]),
        out_shape=jax.ShapeDtypeStruct((p_rows, d), BF16),
        compiler_params=_cparams(1),
        name="moe_gather",
    )(row_src, xs)


def _first_tile_of_expert(te_ref, i):
    return jnp.logical_or(i == 0, te_ref[i] != te_ref[jnp.maximum(i - 1, 0)])


def _for_valid_rows(rows, o_ref, compute):
    tm = o_ref.shape[0]
    sub = min(MOE_SUB_ROWS, tm)
    for m in range(sub, tm + 1, sub):
        @pl.when(jnp.logical_and(rows > m - sub, rows <= m))
        def _(m=m):
            o_ref[:m, :] = compute(m)
            if m < tm:
                o_ref[m:, :] = jnp.zeros((tm - m, o_ref.shape[1]), o_ref.dtype)

    @pl.when(rows <= 0)
    def _():
        o_ref[...] = jnp.zeros(o_ref.shape, o_ref.dtype)


def _moe_swiglu_body(te_ref, tr_ref, nl_ref, x_ref, wg_ref, wu_ref, o_ref, wgb_ref, wub_ref):
    i = pl.program_id(1)

    def emit(dot):
        def compute(m):
            x = x_ref[:m, :]
            g = dot(x, wg_ref, wgb_ref)
            u = dot(x, wu_ref, wub_ref)
            return _swiglu_act(g, u).astype(o_ref.dtype)

        _for_valid_rows(tr_ref[i], o_ref, compute)

    _with_weights(_first_tile_of_expert(te_ref, i), emit)


def _moe_down_body(te_ref, tr_ref, nl_ref, x_ref, w_ref, o_ref, wb_ref):
    i = pl.program_id(1)

    def emit(dot):
        _for_valid_rows(tr_ref[i], o_ref, lambda m: dot(x_ref[:m, :], w_ref, wb_ref))

    _with_weights(_first_tile_of_expert(te_ref, i), emit)


def _live_tile(i, nl):
    return jnp.minimum(i, nl[0] - 1)


def _moe_swiglu(xg, w_gate_up, groups, tm):
    p_rows, k = xg.shape
    f = w_gate_up.shape[2] // 2
    tn = _tile(f, COL_TILE)
    nj = f // tn
    tile_expert, tile_rows, n_live = _tile_table(groups, p_rows, tm)
    return pl.pallas_call(
        _moe_swiglu_body,
        grid_spec=pltpu.PrefetchScalarGridSpec(
            num_scalar_prefetch=3,
            grid=(nj, p_rows // tm),
            in_specs=[pl.BlockSpec((tm, k), lambda j, i, te, tr, nl: (_live_tile(i, nl), 0)),
                      pl.BlockSpec((None, k, tn), lambda j, i, te, tr, nl: (te[i], 0, j)),
                      pl.BlockSpec((None, k, tn), lambda j, i, te, tr, nl: (te[i], 0, nj + j))],
            out_specs=pl.BlockSpec((tm, tn), lambda j, i, te, tr, nl: (i, j)),
            scratch_shapes=[pltpu.VMEM((k, tn), BF16), pltpu.VMEM((k, tn), BF16)]),
        out_shape=jax.ShapeDtypeStruct((p_rows, f), BF16),
        compiler_params=_cparams(2),
        name="moe_swiglu",
    )(tile_expert, tile_rows, n_live, xg, w_gate_up, w_gate_up)


def _moe_down(act, w_down, groups, tm):
    p_rows, k = act.shape
    n = w_down.shape[2]
    tn = _tile(n, COL_TILE)
    tile_expert, tile_rows, n_live = _tile_table(groups, p_rows, tm)
    return pl.pallas_call(
        _moe_down_body,
        grid_spec=pltpu.PrefetchScalarGridSpec(
            num_scalar_prefetch=3,
            grid=(n // tn, p_rows // tm),
            in_specs=[pl.BlockSpec((tm, k), lambda j, i, te, tr, nl: (_live_tile(i, nl), 0)),
                      pl.BlockSpec((None, k, tn), lambda j, i, te, tr, nl: (te[i], 0, j))],
            out_specs=pl.BlockSpec((tm, tn), lambda j, i, te, tr, nl: (i, j)),
            scratch_shapes=[pltpu.VMEM((k, tn), BF16)]),
        out_shape=jax.ShapeDtypeStruct((p_rows, n), F32),
        compiler_params=_cparams(2),
        name="moe_down",
    )(tile_expert, tile_rows, n_live, act, w_down)


def _combine_body(dest_ref, h_ref, w_ref, y_hbm, o_ref, ya0_ref, yb0_ref, ya1_ref, yb1_ref, sems):
    tc = h_ref.shape[0]
    i = pl.program_id(0)
    bufs = ((ya0_ref, yb0_ref), (ya1_ref, yb1_ref))

    def start_tile(tile, slot):
        def start(g, carry):
            for u in range(DMA_ISSUE_UNROLL):
                r = g * DMA_ISSUE_UNROLL + u
                pair = (tile * tc + r) * EXPERT_TOP_K
                for c in range(EXPERT_TOP_K):
                    pltpu.make_async_copy(y_hbm.at[pl.ds(dest_ref[pair + c], 1), :],
                                          bufs[slot][c].at[pl.ds(r, 1), :],
                                          sems.at[slot, c]).start(priority=c % 2)
            return carry

        lax.fori_loop(0, tc // DMA_ISSUE_UNROLL, start, 0)

    def finish_tile(slot):
        for c in range(EXPERT_TOP_K):
            pltpu.make_async_copy(y_hbm.at[pl.ds(0, tc), :], bufs[slot][c], sems.at[slot, c]).wait()
        w = w_ref[...]
        o_ref[...] = h_ref[...] + (w[:, 0:1] * bufs[slot][0][...] + w[:, 1:2] * bufs[slot][1][...])

    @pl.when(i == 0)
    def _():
        start_tile(0, 0)

    for slot in (0, 1):
        @pl.when(jnp.logical_and(i % 2 == slot, i + 1 < pl.num_programs(0)))
        def _(slot=slot):
            start_tile(i + 1, 1 - slot)

        @pl.when(i % 2 == slot)
        def _(slot=slot):
            finish_tile(slot)


def _combine(h, top_w, y, dest):
    n, d = h.shape
    tc = _tile(n, GATHER_ROW_TILE, 8)
    return pl.pallas_call(
        _combine_body,
        grid_spec=pltpu.PrefetchScalarGridSpec(
            num_scalar_prefetch=1,
            grid=(n // tc,),
            in_specs=[pl.BlockSpec((tc, d), lambda i, dst: (i, 0)),
                      pl.BlockSpec((tc, EXPERT_TOP_K), lambda i, dst: (i, 0)),
                      pl.BlockSpec(memory_space=pl.ANY)],
            out_specs=pl.BlockSpec((tc, d), lambda i, dst: (i, 0)),
            scratch_shapes=[pltpu.VMEM((tc, d), F32) for _ in range(2 * EXPERT_TOP_K)]
            + [pltpu.SemaphoreType.DMA((2, EXPERT_TOP_K))]),
        out_shape=jax.ShapeDtypeStruct((n, d), F32),
        compiler_params=_cparams(1),
        name="moe_combine",
    )(dest, h, top_w, y)


def kernel(x, attn_norm, w_qkv, q_norm, k_norm, w_attn_out, ffn_norm, w_ffn_gate_up, w_ffn_down,
           conv_norm, w_conv_in, conv_w, w_conv_out, moe_norm, w_router, w_expert_gate_up,
           w_expert_down):
    batch, seq, d = x.shape
    head_dim = q_norm.shape[-1]
    n_heads = d // head_dim
    n_experts = w_router.shape[-1]
    assert attn_norm.shape[0] == 1 and conv_norm.shape[0] == 1, "two-layer trunk: attention, then conv"
    n_tok = batch * seq
    h = x.reshape(n_tok, d)

    gain = jnp.concatenate([jnp.tile(q_norm[0] * (LOG2E / math.sqrt(head_dim)), n_heads),
                            jnp.tile(k_norm[0], n_heads),
                            jnp.ones((d,), F32)]).reshape(1, 3 * d)
    xn = _rmsnorm(h, attn_norm[0])
    qkv = _mm(xn, w_qkv[0], qk_gain=gain, n_norm_cols=2 * d, head_dim=head_dim)
    o = _moba_attention(qkv, batch, seq, n_heads, head_dim)
    h = _mm(o, w_attn_out[0], residual=h)
    xn = _rmsnorm(h, ffn_norm[0])
    act = _mm_swiglu(xn, w_ffn_gate_up[0])
    h = _mm_long_k_residual(act, w_ffn_down[0], h)

    xn = _rmsnorm(h, conv_norm[0])
    bch = _mm(xn, w_conv_in[0])
    z = _conv_gate(bch, conv_w[0], seq)
    h = _mm(z, w_conv_out[0], residual=h)
    xs, top_idx, top_w = _router(h, moe_norm[0], w_router[0])
    tm = _tile(n_tok, MOE_ROW_TILE, BF16_SUBLANES)
    dest, row_src, groups = _group_by_expert(top_idx, n_experts, tm)
    xg = _gather_rows(xs, row_src, d)
    act = _moe_swiglu(xg, w_expert_gate_up[0], groups, tm)
    y = _moe_down(act, w_expert_down[0], groups, tm)
    h = _combine(h, top_w, y, dest)
    return h.reshape(batch, seq, d)
```

```python
import functools
import math

import numpy as np
import jax
import jax.numpy as jnp
from jax import lax
from jax.experimental import pallas as pl
from jax.experimental.pallas import tpu as pltpu

F32 = jnp.float32
BF16 = jnp.bfloat16

MOBA_BLOCK = 256
MOBA_TOP_K = 3
EXPERT_TOP_K = 2
CONV_WIDTH = 3
RMS_EPS = 1e-6
NEG = -1e30

LANES = 128
BF16_SUBLANES = 16
V7X_VMEM_BYTES = 64 * 1024 * 1024
VMEM_LIMIT_BYTES = V7X_VMEM_BYTES - 8 * 1024 * 1024

ROW_TILE = 1024
COL_TILE = 512
MAX_RESIDENT_K = 4096
MOE_ROW_TILE = 512
MOE_SUB_ROWS = 256
NORM_ROW_TILE = 256
GATHER_ROW_TILE = 256
DMA_ISSUE_UNROLL = 8
SLAB_ROWS = 40
FRESH_K_SLAB = 512


def _tile(dim, pref, mult=LANES):
    if dim <= pref:
        return dim
    t = pref - pref % mult
    while t >= mult:
        if dim % t == 0:
            return t
        t -= mult
    return dim


def _cparams(n_axes):
    return pltpu.CompilerParams(dimension_semantics=("arbitrary",) * n_axes,
                                vmem_limit_bytes=VMEM_LIMIT_BYTES)


def _dot_fresh(x, w_ref, wb_ref):
    k = w_ref.shape[0]
    kc = _tile(k, FRESH_K_SLAB, BF16_SUBLANES)
    acc = None
    for c in range(k // kc):
        sl = slice(c * kc, (c + 1) * kc)
        wb_ref[sl, :] = w_ref[sl, :].astype(wb_ref.dtype)
        part = jnp.dot(x[:, sl], wb_ref[sl, :], preferred_element_type=F32)
        acc = part if acc is None else acc + part
    return acc


def _dot_converted(x, w_ref, wb_ref):
    return jnp.dot(x, wb_ref[...], preferred_element_type=F32)


def _with_weights(fresh, emit):
    @pl.when(fresh)
    def _():
        emit(_dot_fresh)

    @pl.when(jnp.logical_not(fresh))
    def _():
        emit(_dot_converted)


def _rmsnorm_body(x_ref, g_ref, o_ref):
    x = x_ref[...]
    inv = lax.rsqrt(jnp.mean(x * x, axis=-1, keepdims=True) + RMS_EPS)
    o_ref[...] = (x * inv * g_ref[...]).astype(o_ref.dtype)


def _rmsnorm(x, g):
    n, d = x.shape
    tm = _tile(n, NORM_ROW_TILE, BF16_SUBLANES)
    return pl.pallas_call(
        _rmsnorm_body,
        grid=(n // tm,),
        in_specs=[pl.BlockSpec((tm, d), lambda i: (i, 0)),
                  pl.BlockSpec((1, d), lambda i: (0, 0))],
        out_specs=pl.BlockSpec((tm, d), lambda i: (i, 0)),
        out_shape=jax.ShapeDtypeStruct((n, d), BF16),
        compiler_params=_cparams(1),
        name="rmsnorm",
    )(x, g.reshape(1, d))


def _mm_plain_body(x_ref, w_ref, o_ref, wb_ref):
    def emit(dot):
        o_ref[...] = dot(x_ref[...], w_ref, wb_ref).astype(o_ref.dtype)

    _with_weights(pl.program_id(1) == 0, emit)


def _mm_residual_body(x_ref, w_ref, r_ref, o_ref, wb_ref):
    def emit(dot):
        o_ref[...] = r_ref[...] + dot(x_ref[...], w_ref, wb_ref)

    _with_weights(pl.program_id(1) == 0, emit)


def _mm_qknorm_body(x_ref, w_ref, g_ref, o_ref, wb_ref, *, n_norm_tiles, head_dim):
    j = pl.program_id(0)

    def emit(dot):
        acc = dot(x_ref[...], w_ref, wb_ref)

        @pl.when(j < n_norm_tiles)
        def _():
            for hh in range(acc.shape[1] // head_dim):
                sl = slice(hh * head_dim, (hh + 1) * head_dim)
                a = acc[:, sl]
                inv = lax.rsqrt(jnp.mean(a * a, axis=-1, keepdims=True) + RMS_EPS)
                o_ref[:, sl] = (a * inv * g_ref[:, sl]).astype(o_ref.dtype)

        @pl.when(j >= n_norm_tiles)
        def _():
            o_ref[...] = acc.astype(o_ref.dtype)

    _with_weights(pl.program_id(1) == 0, emit)


def _swiglu_act(g, u):
    return g * (1.0 / (1.0 + jnp.exp(-g))) * u


def _mm_swiglu_body(x_ref, wg_ref, wu_ref, o_ref, wgb_ref, wub_ref):
    def emit(dot):
        x = x_ref[...]
        g = dot(x, wg_ref, wgb_ref)
        u = dot(x, wu_ref, wub_ref)
        o_ref[...] = _swiglu_act(g, u).astype(o_ref.dtype)

    _with_weights(pl.program_id(1) == 0, emit)


def _mm(x, w, *, out_dtype=BF16, residual=None, qk_gain=None, n_norm_cols=0, head_dim=LANES,
        k_blocks=1, k_block=0):
    m = x.shape[0]
    k = x.shape[1] // k_blocks
    n = w.shape[1]
    tm = _tile(m, ROW_TILE, BF16_SUBLANES)
    tn = _tile(n, COL_TILE)
    grid = (n // tn, m // tm)
    in_specs = [pl.BlockSpec((tm, k), lambda j, i: (i, k_block)),
                pl.BlockSpec((k, tn), lambda j, i: (k_block, j))]
    args = [x, w]
    if residual is not None:
        body = _mm_residual_body
        in_specs.append(pl.BlockSpec((tm, tn), lambda j, i: (i, j)))
        args.append(residual)
        out_dtype = F32
        name = "mm_residual"
    elif qk_gain is not None:
        assert n_norm_cols % tn == 0 and tn % head_dim == 0
        body = functools.partial(_mm_qknorm_body, n_norm_tiles=n_norm_cols // tn, head_dim=head_dim)
        in_specs.append(pl.BlockSpec((1, tn), lambda j, i: (0, j)))
        args.append(qk_gain)
        name = "mm_qknorm"
    else:
        body = _mm_plain_body
        name = "mm_plain"
    return pl.pallas_call(
        body,
        grid=grid,
        in_specs=in_specs,
        out_specs=pl.BlockSpec((tm, tn), lambda j, i: (i, j)),
        out_shape=jax.ShapeDtypeStruct((m, n), out_dtype),
        scratch_shapes=[pltpu.VMEM((k, tn), BF16)],
        compiler_params=_cparams(2),
        name=name,
    )(*args)


def _mm_swiglu(x, w_gate_up):
    m, k = x.shape
    f = w_gate_up.shape[1] // 2
    tm = _tile(m, ROW_TILE, BF16_SUBLANES)
    tn = _tile(f, COL_TILE // 2)
    nj = f // tn
    return pl.pallas_call(
        _mm_swiglu_body,
        grid=(nj, m // tm),
        in_specs=[pl.BlockSpec((tm, k), lambda j, i: (i, 0)),
                  pl.BlockSpec((k, tn), lambda j, i: (0, j)),
                  pl.BlockSpec((k, tn), lambda j, i: (0, nj + j))],
        out_specs=pl.BlockSpec((tm, tn), lambda j, i: (i, j)),
        out_shape=jax.ShapeDtypeStruct((m, f), BF16),
        scratch_shapes=[pltpu.VMEM((k, tn), BF16), pltpu.VMEM((k, tn), BF16)],
        compiler_params=_cparams(2),
        name="mm_swiglu",
    )(x, w_gate_up, w_gate_up)


def _mm_long_k_residual(x, w, residual):
    k = x.shape[1]
    k_blocks = 1
    while k // k_blocks > MAX_RESIDENT_K or k % k_blocks or (k // k_blocks) % LANES:
        k_blocks += 1
    for b in range(k_blocks):
        residual = _mm(x, w, residual=residual, k_blocks=k_blocks, k_block=b)
    return residual


LOG2E = math.log2(math.e)
SLOPE_TERMS = 3


def _col_reduce(x, op):
    rows = x.shape[0]
    while rows > 8 and rows % 16 == 0:
        rows //= 2
        x = op(x[:rows, :], x[rows:, :])
    red = jnp.max if op is jnp.maximum else jnp.sum
    return red(x, axis=0, keepdims=True)


def _attn_body(qc_ref, q_ref, k_ref, v_ref, o_ref, ka_ref, qa_ref, vt_ref, *, n_blocks, top_k):
    blk = MOBA_BLOCK
    blk_shift = int(math.log2(blk))
    s_len = n_blocks * blk
    hd = q_ref.shape[1]
    nt_dims = (((1,), (1,)), ((), ()))

    km = jnp.concatenate(
        [jnp.mean(k_ref[n * blk:(n + 1) * blk, :].astype(F32), axis=0, keepdims=True)
         for n in range(n_blocks)], axis=0)
    km_hi = km.astype(BF16)
    km_lo = (km - km_hi.astype(F32)).astype(BF16)
    km2 = jnp.concatenate([km_hi, km_lo], axis=1)
    q = q_ref[...]
    q2 = jnp.concatenate([q, q], axis=1)
    gate = lax.dot_general(km2, q2, nt_dims, preferred_element_type=F32)

    n_iota = lax.broadcasted_iota(jnp.int32, (n_blocks, s_len), 0)
    q_blk = lax.shift_right_logical(lax.broadcasted_iota(jnp.int32, (n_blocks, s_len), 1), blk_shift)
    valid = n_iota < q_blk
    g = jnp.where(valid, gate, -jnp.inf)
    rank = jnp.zeros((n_blocks, s_len), jnp.int32)
    for mm in range(n_blocks):
        gm = g[mm:mm + 1, :]
        tie = jnp.where(n_iota > mm, 1, 0)
        rank = rank + jnp.where(gm > g, 1, jnp.where(gm == g, tie, 0))
    sel = jnp.where(valid, jnp.where(rank < top_k, 0.0, NEG),
                    jnp.where(n_iota == q_blk, 0.0, NEG))
    sel_t = jnp.concatenate([sel, jnp.zeros((LANES - n_blocks, s_len), F32)], axis=0).T

    qa_ref[:, :hd] = q
    qa_ref[:, hd:] = (sel_t + qc_ref[...]).astype(BF16)
    t = lax.broadcasted_iota(jnp.int32, (s_len, LANES), 0)
    lane = lax.broadcasted_iota(jnp.int32, (s_len, LANES), 1)
    t_blk = lax.shift_right_logical(t, blk_shift)
    t_off = jnp.bitwise_and(t, blk - 1)
    k_cols = jnp.where(lane < n_blocks, jnp.where(lane == t_blk, 1.0, 0.0),
                       jnp.where(lane < n_blocks + SLOPE_TERMS, t_off.astype(F32),
                                 jnp.where(lane < n_blocks + 2 * SLOPE_TERMS, t_blk.astype(F32), 0.0)))
    ka_ref[:, :hd] = k_ref[...]
    ka_ref[:, hd:] = k_cols.astype(BF16)

    vt_ref[...] = v_ref[...].astype(F32).T.astype(vt_ref.dtype)

    r = lax.broadcasted_iota(jnp.int32, (blk, blk), 1)
    c = lax.broadcasted_iota(jnp.int32, (blk, blk), 0)
    causal = c <= r

    def scores(i):
        return lax.dot_general(ka_ref[:(i + 1) * blk, :], qa_ref[i * blk:(i + 1) * blk, :], nt_dims,
                               preferred_element_type=F32)

    ahead = 2
    s_queue = [scores(i) for i in range(min(ahead, n_blocks))]
    for i in range(n_blocks):
        sl = slice(i * blk, (i + 1) * blk)
        n_keys = (i + 1) * blk
        s = s_queue.pop(0)
        if i + ahead < n_blocks:
            s_queue.append(scores(i + ahead))
        s_own = jnp.where(causal, s[i * blk:, :], NEG)
        m = _col_reduce(s_own, jnp.maximum)
        if i > 0:
            s_past = s[:i * blk, :]
            m = jnp.maximum(m, _col_reduce(s_past, jnp.maximum))
        p_own = jnp.exp2(s_own - m)
        l = _col_reduce(p_own, jnp.add)
        p = p_own.astype(BF16)
        if i > 0:
            p_past = jnp.exp2(s_past - m)
            l = l + _col_reduce(p_past, jnp.add)
            p = jnp.concatenate([p_past.astype(BF16), p], axis=0)
        acc = jnp.dot(vt_ref[:, :n_keys], p, preferred_element_type=F32)
        o_ref[sl, :] = (acc * (1.0 / l)).T.astype(o_ref.dtype)


def _moba_attention(qkv, batch, seq, n_heads, head_dim):
    assert seq % MOBA_BLOCK == 0 and head_dim == LANES
    n_blocks = seq // MOBA_BLOCK
    assert n_blocks + 2 * SLOPE_TERMS <= LANES
    top_k = min(MOBA_TOP_K, n_blocks - 1)
    rest = jnp.asarray(np.power(2.0, -8.0 * np.arange(1, n_heads + 1) / n_heads).astype(np.float32)) * LOG2E
    terms = []
    for _ in range(SLOPE_TERMS):
        term = rest.astype(BF16).astype(F32)
        terms.append(term)
        rest = rest - term
    cols = jnp.stack(terms + [MOBA_BLOCK * term for term in terms], axis=1)
    q_cols = jnp.zeros((n_heads, 1, LANES), F32).at[:, 0, n_blocks:n_blocks + 2 * SLOPE_TERMS].set(cols)
    blk_spec = lambda off: pl.BlockSpec((seq, head_dim), lambda b, h: (b, off + h))
    return pl.pallas_call(
        functools.partial(_attn_body, n_blocks=n_blocks, top_k=top_k),
        grid=(batch, n_heads),
        in_specs=[pl.BlockSpec((None, 1, LANES), lambda b, h: (h, 0, 0)),
                  blk_spec(0), blk_spec(n_heads), blk_spec(2 * n_heads)],
        out_specs=pl.BlockSpec((seq, head_dim), lambda b, h: (b, h)),
        out_shape=jax.ShapeDtypeStruct((batch * seq, n_heads * head_dim), BF16),
        scratch_shapes=[pltpu.VMEM((seq, 2 * head_dim), BF16),
                        pltpu.VMEM((seq, 2 * head_dim), BF16),
                        pltpu.VMEM((head_dim, seq), BF16)],
        compiler_params=_cparams(2),
        name="moba_attention",
    )(q_cols, qkv, qkv, qkv)


def _conv_gate_body(b_ref, c_ref, h_ref, cp_ref, hp_ref, w_ref, o_ref, *, seq):
    tm = c_ref.shape[0]
    u = c_ref[...].astype(F32) * h_ref[...].astype(F32)
    keep = jnp.where((pl.program_id(0) * tm) % seq == 0, 0.0, 1.0)
    up = cp_ref[...].astype(F32) * hp_ref[...].astype(F32) * keep
    last = up.shape[0] - 1
    row = lax.broadcasted_iota(jnp.int32, u.shape, 0)
    u1 = jnp.where(row == 0, up[last:last + 1, :], pltpu.roll(u, 1, 0))
    u2 = jnp.where(row == 0, up[last - 1:last, :],
                   jnp.where(row == 1, up[last:last + 1, :], pltpu.roll(u, 2, 0)))
    y = w_ref[0:1, :] * u2 + w_ref[1:2, :] * u1 + w_ref[2:3, :] * u
    o_ref[...] = (b_ref[...].astype(F32) * y).astype(o_ref.dtype)


def _conv_gate(bch, conv_w, seq):
    n, d3 = bch.shape
    d = d3 // 3
    tm = _tile(seq, 512, BF16_SUBLANES)
    tc = _tile(d, 1024)
    nc = d // tc
    halo = BF16_SUBLANES
    hb = tm // halo
    prev = lambda off: (lambda i, j: (jnp.maximum(i * hb - 1, 0), off + j))
    return pl.pallas_call(
        functools.partial(_conv_gate_body, seq=seq),
        grid=(n // tm, nc),
        in_specs=[pl.BlockSpec((tm, tc), lambda i, j: (i, j)),
                  pl.BlockSpec((tm, tc), lambda i, j: (i, nc + j)),
                  pl.BlockSpec((tm, tc), lambda i, j: (i, 2 * nc + j)),
                  pl.BlockSpec((halo, tc), prev(nc)),
                  pl.BlockSpec((halo, tc), prev(2 * nc)),
                  pl.BlockSpec((CONV_WIDTH, tc), lambda i, j: (0, j))],
        out_specs=pl.BlockSpec((tm, tc), lambda i, j: (i, j)),
        out_shape=jax.ShapeDtypeStruct((n, d), BF16),
        compiler_params=_cparams(2),
        name="conv_gate",
    )(bch, bch, bch, bch, bch, conv_w)


def _router_body(x_ref, g_ref, wr_ref, xs_ref, idx_ref, wgt_ref, *, n_experts):
    x = x_ref[...]
    y = x * lax.rsqrt(jnp.mean(x * x, axis=-1, keepdims=True) + RMS_EPS) * g_ref[...]
    tm, d = y.shape
    for s in range(d // LANES):
        xs_ref[pl.ds(s, tm, stride=SLAB_ROWS), :] = y[:, s * LANES:(s + 1) * LANES]
    for s in range(d // LANES, SLAB_ROWS):
        xs_ref[pl.ds(s, tm, stride=SLAB_ROWS), :] = jnp.zeros((tm, LANES), F32)
    y_hi = y.astype(BF16)
    y_lo = (y - y_hi.astype(F32)).astype(BF16)
    wr = wr_ref[...]
    wr_hi = wr.astype(BF16)
    wr_lo = (wr - wr_hi.astype(F32)).astype(BF16)
    logits = (jnp.dot(y_hi, wr_hi, preferred_element_type=F32)
              + (jnp.dot(y_hi, wr_lo, preferred_element_type=F32)
                 + jnp.dot(y_lo, wr_hi, preferred_element_type=F32)))
    lane = lax.broadcasted_iota(jnp.int32, logits.shape, 1).astype(F32)
    big = float(logits.shape[1])
    lg = jnp.where(lane < n_experts, logits, -jnp.inf)
    m1 = jnp.max(lg, axis=-1, keepdims=True)
    i1 = jnp.min(jnp.where(lg == m1, lane, big), axis=-1, keepdims=True)
    lg2 = jnp.where(lane == i1, -jnp.inf, lg)
    m2 = jnp.max(lg2, axis=-1, keepdims=True)
    i2 = jnp.min(jnp.where(lg2 == m2, lane, big), axis=-1, keepdims=True)
    e = jnp.exp(m2 - m1)
    w1 = 1.0 / (1.0 + e)
    w2 = e * w1
    k = idx_ref.shape[1]
    idx_ref[...] = jnp.where(lane == 0.0, i1, i2)[:, :k].astype(jnp.int32)
    wgt_ref[...] = jnp.where(lane == 0.0, w1, w2)[:, :k]


def _router(h, g, w_router):
    n, d = h.shape
    assert d % LANES == 0 and d // LANES <= SLAB_ROWS
    n_experts = w_router.shape[1]
    tm = _tile(n, NORM_ROW_TILE, 8)
    wr = jnp.pad(w_router, ((0, 0), (0, LANES - n_experts)))
    return pl.pallas_call(
        functools.partial(_router_body, n_experts=n_experts),
        grid=(n // tm,),
        in_specs=[pl.BlockSpec((tm, d), lambda i: (i, 0)),
                  pl.BlockSpec((1, d), lambda i: (0, 0)),
                  pl.BlockSpec((d, LANES), lambda i: (0, 0))],
        out_specs=[pl.BlockSpec((tm * SLAB_ROWS, LANES), lambda i: (i, 0)),
                   pl.BlockSpec((tm, EXPERT_TOP_K), lambda i: (i, 0)),
                   pl.BlockSpec((tm, EXPERT_TOP_K), lambda i: (i, 0))],
        out_shape=[jax.ShapeDtypeStruct((n * SLAB_ROWS, LANES), F32),
                   jax.ShapeDtypeStruct((n, EXPERT_TOP_K), jnp.int32),
                   jax.ShapeDtypeStruct((n, EXPERT_TOP_K), F32)],
        compiler_params=_cparams(1),
        name="router",
    )(h, g.reshape(1, d), wr)


def _group_by_expert(top_idx, n_experts, align):
    n = top_idx.shape[0]
    n_pairs = n * EXPERT_TOP_K
    e_flat = top_idx.reshape(-1)
    onehot = (e_flat[:, None] == jnp.arange(n_experts, dtype=jnp.int32)[None, :]).astype(jnp.int32)
    csum = jnp.cumsum(onehot, axis=0)
    rank = jnp.sum(onehot * csum, axis=1) - 1
    counts = csum[-1]
    padded = (counts + align - 1) // align * align
    ends = jnp.cumsum(padded)
    starts = ends - padded
    valid_starts = ends - counts
    dest = (jnp.sum(onehot * valid_starts[None, :], axis=1) + rank).astype(jnp.int32)
    p_rows = n_pairs + n_experts * align
    row_src = jnp.zeros((p_rows,), jnp.int32).at[dest].set(
        jnp.arange(n_pairs, dtype=jnp.int32) // EXPERT_TOP_K)
    return dest, row_src, (valid_starts, ends)


def _tile_table(groups, p_rows, tm):
    valid_starts, ends = groups
    n_experts = ends.shape[0]
    tile_start = jnp.arange(p_rows // tm, dtype=jnp.int32) * tm
    before = jnp.sum((tile_start[:, None] >= ends[None, :]).astype(jnp.int32), axis=1)
    live = before < n_experts
    n_live = ends[-1] // tm
    expert = jnp.minimum(before, n_experts - 1)
    rows = jnp.where(live, jnp.clip(tile_start + tm - valid_starts[expert], 0, tm), 0)
    expert = jnp.where(live, expert, expert[jnp.maximum(n_live - 1, 0)])
    return expert.astype(jnp.int32), rows.astype(jnp.int32), n_live.reshape(1).astype(jnp.int32)


def _gather_rows_body(src_ref, x_hbm, o_ref, buf0_ref, buf1_ref, sems):
    tg, d = o_ref.shape
    n_slab = d // LANES
    i = pl.program_id(0)
    bufs = (buf0_ref, buf1_ref)

    def start_tile(tile, slot):
        def start(g, carry):
            for u in range(DMA_ISSUE_UNROLL):
                r = g * DMA_ISSUE_UNROLL + u
                src = pl.multiple_of(src_ref[tile * tg + r] * SLAB_ROWS, SLAB_ROWS)
                dst = pl.multiple_of(r * SLAB_ROWS, SLAB_ROWS)
                pltpu.make_async_copy(x_hbm.at[pl.ds(src, n_slab), :],
                                      bufs[slot].at[pl.ds(dst, n_slab), :],
                                      sems.at[slot]).start(priority=u % 2)
            return carry

        lax.fori_loop(0, tg // DMA_ISSUE_UNROLL, start, 0)

    def finish_tile(slot):
        pltpu.make_async_copy(x_hbm.at[pl.ds(0, tg * n_slab), :],
                              bufs[slot].at[pl.ds(0, tg * n_slab), :], sems.at[slot]).wait()
        for s in range(n_slab):
            o_ref[:, s * LANES:(s + 1) * LANES] = (
                bufs[slot][pl.ds(s, tg, stride=SLAB_ROWS), :].astype(o_ref.dtype))

    @pl.when(i == 0)
    def _():
        start_tile(0, 0)

    for slot in (0, 1):
        @pl.when(jnp.logical_and(i % 2 == slot, i + 1 < pl.num_programs(0)))
        def _(slot=slot):
            start_tile(i + 1, 1 - slot)

        @pl.when(i % 2 == slot)
        def _(slot=slot):
            finish_tile(slot)


def _gather_rows(xs, row_src, d):
    p_rows = row_src.shape[0]
    tg = _tile(p_rows, GATHER_ROW_TILE, BF16_SUBLANES)
    return pl.pallas_call(
        _gather_rows_body,
        grid_spec=pltpu.PrefetchScalarGridSpec(
            num_scalar_prefetch=1,
            grid=(p_rows // tg,),
            in_specs=[pl.BlockSpec(memory_space=pl.ANY)],
            out_specs=pl.BlockSpec((tg, d), lambda i, src: (i, 0)),
            scratch_shapes=[pltpu.VMEM((tg * SLAB_ROWS, LANES), F32),
                            pltpu.VMEM((tg * SLAB_ROWS, LANES), F32),
                            pltpu.SemaphoreType.DMA((2,))]),
        out_shape=jax.ShapeDtypeStruct((p_rows, d), BF16),
        compiler_params=_cparams(1),
        name="moe_gather",
    )(row_src, xs)


def _first_tile_of_expert(te_ref, i):
    return jnp.logical_or(i == 0, te_ref[i] != te_ref[jnp.maximum(i - 1, 0)])


def _for_valid_rows(rows, o_ref, compute):
    tm = o_ref.shape[0]
    sub = min(MOE_SUB_ROWS, tm)
    for m in range(sub, tm + 1, sub):
        @pl.when(jnp.logical_and(rows > m - sub, rows <= m))
        def _(m=m):
            o_ref[tm - m:, :] = compute(tm - m)
            if m < tm:
                o_ref[:tm - m, :] = jnp.zeros((tm - m, o_ref.shape[1]), o_ref.dtype)

    @pl.when(rows <= 0)
    def _():
        o_ref[...] = jnp.zeros(o_ref.shape, o_ref.dtype)


def _moe_swiglu_body(te_ref, tr_ref, nl_ref, x_ref, wg_ref, wu_ref, o_ref, wgb_ref, wub_ref):
    i = pl.program_id(1)

    def emit(dot):
        def compute(row0):
            x = x_ref[row0:, :]
            g = dot(x, wg_ref, wgb_ref)
            u = dot(x, wu_ref, wub_ref)
            return _swiglu_act(g, u).astype(o_ref.dtype)

        _for_valid_rows(tr_ref[i], o_ref, compute)

    _with_weights(_first_tile_of_expert(te_ref, i), emit)


def _moe_down_body(te_ref, tr_ref, nl_ref, x_ref, w_ref, o_ref, wb_ref):
    i = pl.program_id(1)

    def emit(dot):
        _for_valid_rows(tr_ref[i], o_ref, lambda row0: dot(x_ref[row0:, :], w_ref, wb_ref))

    _with_weights(_first_tile_of_expert(te_ref, i), emit)


def _live_tile(i, nl):
    return jnp.minimum(i, nl[0] - 1)


def _moe_swiglu(xg, w_gate_up, groups, tm):
    p_rows, k = xg.shape
    f = w_gate_up.shape[2] // 2
    tn = _tile(f, COL_TILE)
    nj = f // tn
    tile_expert, tile_rows, n_live = _tile_table(groups, p_rows, tm)
    return pl.pallas_call(
        _moe_swiglu_body,
        grid_spec=pltpu.PrefetchScalarGridSpec(
            num_scalar_prefetch=3,
            grid=(nj, p_rows // tm),
            in_specs=[pl.BlockSpec((tm, k), lambda j, i, te, tr, nl: (_live_tile(i, nl), 0)),
                      pl.BlockSpec((None, k, tn), lambda j, i, te, tr, nl: (te[i], 0, j)),
                      pl.BlockSpec((None, k, tn), lambda j, i, te, tr, nl: (te[i], 0, nj + j))],
            out_specs=pl.BlockSpec((tm, tn), lambda j, i, te, tr, nl: (i, j)),
            scratch_shapes=[pltpu.VMEM((k, tn), BF16), pltpu.VMEM((k, tn), BF16)]),
        out_shape=jax.ShapeDtypeStruct((p_rows, f), BF16),
        compiler_params=_cparams(2),
        name="moe_swiglu",
    )(tile_expert, tile_rows, n_live, xg, w_gate_up, w_gate_up)


def _moe_down(act, w_down, groups, tm):
    p_rows, k = act.shape
    n = w_down.shape[2]
    tn = _tile(n, COL_TILE)
    tile_expert, tile_rows, n_live = _tile_table(groups, p_rows, tm)
    return pl.pallas_call(
        _moe_down_body,
        grid_spec=pltpu.PrefetchScalarGridSpec(
            num_scalar_prefetch=3,
            grid=(n // tn, p_rows // tm),
            in_specs=[pl.BlockSpec((tm, k), lambda j, i, te, tr, nl: (_live_tile(i, nl), 0)),
                      pl.BlockSpec((None, k, tn), lambda j, i, te, tr, nl: (te[i], 0, j))],
            out_specs=pl.BlockSpec((tm, tn), lambda j, i, te, tr, nl: (i, j)),
            scratch_shapes=[pltpu.VMEM((k, tn), BF16)]),
        out_shape=jax.ShapeDtypeStruct((p_rows, n), F32),
        compiler_params=_cparams(2),
        name="moe_down",
    )(tile_expert, tile_rows, n_live, act, w_down)


def _combine_body(dest_ref, h_ref, w_ref, y_hbm, o_ref, ya0_ref, yb0_ref, ya1_ref, yb1_ref, sems):
    tc = h_ref.shape[0]
    i = pl.program_id(0)
    bufs = ((ya0_ref, yb0_ref), (ya1_ref, yb1_ref))

    def start_tile(tile, slot):
        def start(g, carry):
            for u in range(DMA_ISSUE_UNROLL):
                r = g * DMA_ISSUE_UNROLL + u
                pair = (tile * tc + r) * EXPERT_TOP_K
                for c in range(EXPERT_TOP_K):
                    pltpu.make_async_copy(y_hbm.at[pl.ds(dest_ref[pair + c], 1), :],
                                          bufs[slot][c].at[pl.ds(r, 1), :],
                                          sems.at[slot, c]).start(priority=c % 2)
            return carry

        lax.fori_loop(0, tc // DMA_ISSUE_UNROLL, start, 0)

    def finish_tile(slot):
        for c in range(EXPERT_TOP_K):
            pltpu.make_async_copy(y_hbm.at[pl.ds(0, tc), :], bufs[slot][c], sems.at[slot, c]).wait()
        w = w_ref[...]
        o_ref[...] = h_ref[...] + (w[:, 0:1] * bufs[slot][0][...] + w[:, 1:2] * bufs[slot][1][...])

    @pl.when(i == 0)
    def _():
        start_tile(0, 0)

    for slot in (0, 1):
        @pl.when(jnp.logical_and(i % 2 == slot, i + 1 < pl.num_programs(0)))
        def _(slot=slot):
            start_tile(i + 1, 1 - slot)

        @pl.when(i % 2 == slot)
        def _(slot=slot):
            finish_tile(slot)


def _combine(h, top_w, y, dest):
    n, d = h.shape
    tc = _tile(n, GATHER_ROW_TILE, 8)
    return pl.pallas_call(
        _combine_body,
        grid_spec=pltpu.PrefetchScalarGridSpec(
            num_scalar_prefetch=1,
            grid=(n // tc,),
            in_specs=[pl.BlockSpec((tc, d), lambda i, dst: (i, 0)),
                      pl.BlockSpec((tc, EXPERT_TOP_K), lambda i, dst: (i, 0)),
                      pl.BlockSpec(memory_space=pl.ANY)],
            out_specs=pl.BlockSpec((tc, d), lambda i, dst: (i, 0)),
            scratch_shapes=[pltpu.VMEM((tc, d), F32) for _ in range(2 * EXPERT_TOP_K)]
            + [pltpu.SemaphoreType.DMA((2, EXPERT_TOP_K))]),
        out_shape=jax.ShapeDtypeStruct((n, d), F32),
        compiler_params=_cparams(1),
        name="moe_combine",
    )(dest, h, top_w, y)


def kernel(x, attn_norm, w_qkv, q_norm, k_norm, w_attn_out, ffn_norm, w_ffn_gate_up, w_ffn_down,
           conv_norm, w_conv_in, conv_w, w_conv_out, moe_norm, w_router, w_expert_gate_up,
           w_expert_down):
    batch, seq, d = x.shape
    head_dim = q_norm.shape[-1]
    n_heads = d // head_dim
    n_experts = w_router.shape[-1]
    assert attn_norm.shape[0] == 1 and conv_norm.shape[0] == 1, "two-layer trunk: attention, then conv"
    n_tok = batch * seq
    h = x.reshape(n_tok, d)

    gain = jnp.concatenate([jnp.tile(q_norm[0] * (LOG2E / math.sqrt(head_dim)), n_heads),
                            jnp.tile(k_norm[0], n_heads),
                            jnp.ones((d,), F32)]).reshape(1, 3 * d)
    xn = _rmsnorm(h, attn_norm[0])
    qkv = _mm(xn, w_qkv[0], qk_gain=gain, n_norm_cols=2 * d, head_dim=head_dim)
    o = _moba_attention(qkv, batch, seq, n_heads, head_dim)
    h = _mm(o, w_attn_out[0], residual=h)
    xn = _rmsnorm(h, ffn_norm[0])
    act = _mm_swiglu(xn, w_ffn_gate_up[0])
    h = _mm_long_k_residual(act, w_ffn_down[0], h)

    xn = _rmsnorm(h, conv_norm[0])
    bch = _mm(xn, w_conv_in[0])
    z = _conv_gate(bch, conv_w[0], seq)
    h = _mm(z, w_conv_out[0], residual=h)
    xs, top_idx, top_w = _router(h, moe_norm[0], w_router[0])
    tm = _tile(n_tok, MOE_ROW_TILE, BF16_SUBLANES)
    dest, row_src, groups = _group_by_expert(top_idx, n_experts, tm)
    xg = _gather_rows(xs, row_src, d)
    act = _moe_swiglu(xg, w_expert_gate_up[0], groups, tm)
    y = _moe_down(act, w_expert_down[0], groups, tm)
    h = _combine(h, top_w, y, dest)
    return h.reshape(batch, seq, d)
```

```python
import functools
import math

import numpy as np
import jax
import jax.numpy as jnp
from jax import lax
from jax.experimental import pallas as pl
from jax.experimental.pallas import tpu as pltpu

F32 = jnp.float32
BF16 = jnp.bfloat16

MOBA_BLOCK = 256
MOBA_TOP_K = 3
EXPERT_TOP_K = 2
CONV_WIDTH = 3
RMS_EPS = 1e-6
NEG = -1e30

LANES = 128
BF16_SUBLANES = 16
V7X_VMEM_BYTES = 64 * 1024 * 1024
VMEM_LIMIT_BYTES = V7X_VMEM_BYTES - 8 * 1024 * 1024

ROW_TILE = 1024
COL_TILE = 512
MAX_RESIDENT_K = 4096
MOE_ROW_TILE = 512
MOE_SUB_ROWS = 256
NORM_ROW_TILE = 256
GATHER_ROW_TILE = 256
DMA_ISSUE_UNROLL = 8
SLAB_ROWS = 40
FRESH_K_SLAB = 512


def _tile(dim, pref, mult=LANES):
    if dim <= pref:
        return dim
    t = pref - pref % mult
    while t >= mult:
        if dim % t == 0:
            return t
        t -= mult
    return dim


def _cparams(n_axes):
    return pltpu.CompilerParams(dimension_semantics=("arbitrary",) * n_axes,
                                vmem_limit_bytes=VMEM_LIMIT_BYTES)


def _dot_fresh(x, w_ref, wb_ref):
    k = w_ref.shape[0]
    kc = _tile(k, FRESH_K_SLAB, BF16_SUBLANES)
    acc = None
    for c in range(k // kc):
        sl = slice(c * kc, (c + 1) * kc)
        wb_ref[sl, :] = w_ref[sl, :].astype(wb_ref.dtype)
        part = jnp.dot(x[:, sl], wb_ref[sl, :], preferred_element_type=F32)
        acc = part if acc is None else acc + part
    return acc


def _dot_converted(x, w_ref, wb_ref):
    return jnp.dot(x, wb_ref[...], preferred_element_type=F32)


def _with_weights(fresh, emit):
    @pl.when(fresh)
    def _():
        emit(_dot_fresh)

    @pl.when(jnp.logical_not(fresh))
    def _():
        emit(_dot_converted)


def _rmsnorm_body(x_ref, g_ref, o_ref):
    x = x_ref[...]
    inv = lax.rsqrt(jnp.mean(x * x, axis=-1, keepdims=True) + RMS_EPS)
    o_ref[...] = (x * inv * g_ref[...]).astype(o_ref.dtype)


def _rmsnorm(x, g):
    n, d = x.shape
    tm = _tile(n, NORM_ROW_TILE, BF16_SUBLANES)
    return pl.pallas_call(
        _rmsnorm_body,
        grid=(n // tm,),
        in_specs=[pl.BlockSpec((tm, d), lambda i: (i, 0)),
                  pl.BlockSpec((1, d), lambda i: (0, 0))],
        out_specs=pl.BlockSpec((tm, d), lambda i: (i, 0)),
        out_shape=jax.ShapeDtypeStruct((n, d), BF16),
        compiler_params=_cparams(1),
        name="rmsnorm",
    )(x, g.reshape(1, d))


def _snake_row(j, i, n_rows):
    return i + (j % 2) * (n_rows - 1 - 2 * i)


def _mm_plain_body(x_ref, w_ref, o_ref, wb_ref):
    def emit(dot):
        o_ref[...] = dot(x_ref[...], w_ref, wb_ref).astype(o_ref.dtype)

    _with_weights(pl.program_id(1) == 0, emit)


def _mm_residual_body(x_ref, w_ref, r_ref, o_ref, wb_ref):
    def emit(dot):
        o_ref[...] = r_ref[...] + dot(x_ref[...], w_ref, wb_ref)

    _with_weights(pl.program_id(1) == 0, emit)


def _mm_qknorm_body(x_ref, w_ref, g_ref, o_ref, wb_ref, *, n_norm_tiles, head_dim):
    j = pl.program_id(0)

    def emit(dot):
        acc = dot(x_ref[...], w_ref, wb_ref)

        @pl.when(j < n_norm_tiles)
        def _():
            for hh in range(acc.shape[1] // head_dim):
                sl = slice(hh * head_dim, (hh + 1) * head_dim)
                a = acc[:, sl]
                inv = lax.rsqrt(jnp.mean(a * a, axis=-1, keepdims=True) + RMS_EPS)
                o_ref[:, sl] = (a * inv * g_ref[:, sl]).astype(o_ref.dtype)

        @pl.when(j >= n_norm_tiles)
        def _():
            o_ref[...] = acc.astype(o_ref.dtype)

    _with_weights(pl.program_id(1) == 0, emit)


def _swiglu_act(g, u):
    return g * (1.0 / (1.0 + jnp.exp(-g))) * u


def _mm_swiglu_body(x_ref, wg_ref, wu_ref, o_ref, wgb_ref, wub_ref):
    def emit(dot):
        x = x_ref[...]
        g = dot(x, wg_ref, wgb_ref)
        u = dot(x, wu_ref, wub_ref)
        o_ref[...] = _swiglu_act(g, u).astype(o_ref.dtype)

    _with_weights(pl.program_id(1) == 0, emit)


def _mm(x, w, *, out_dtype=BF16, residual=None, qk_gain=None, n_norm_cols=0, head_dim=LANES,
        k_blocks=1, k_block=0):
    m = x.shape[0]
    k = x.shape[1] // k_blocks
    n = w.shape[1]
    tm = _tile(m, ROW_TILE, BF16_SUBLANES)
    tn = _tile(n, COL_TILE)
    grid = (n // tn, m // tm)
    row = functools.partial(_snake_row, n_rows=m // tm)
    in_specs = [pl.BlockSpec((tm, k), lambda j, i: (row(j, i), k_block)),
                pl.BlockSpec((k, tn), lambda j, i: (k_block, j))]
    args = [x, w]
    if residual is not None:
        body = _mm_residual_body
        in_specs.append(pl.BlockSpec((tm, tn), lambda j, i: (row(j, i), j)))
        args.append(residual)
        out_dtype = F32
        name = "mm_residual"
    elif qk_gain is not None:
        assert n_norm_cols % tn == 0 and tn % head_dim == 0
        body = functools.partial(_mm_qknorm_body, n_norm_tiles=n_norm_cols // tn, head_dim=head_dim)
        in_specs.append(pl.BlockSpec((1, tn), lambda j, i: (0, j)))
        args.append(qk_gain)
        name = "mm_qknorm"
    else:
        body = _mm_plain_body
        name = "mm_plain"
    return pl.pallas_call(
        body,
        grid=grid,
        in_specs=in_specs,
        out_specs=pl.BlockSpec((tm, tn), lambda j, i: (row(j, i), j)),
        out_shape=jax.ShapeDtypeStruct((m, n), out_dtype),
        scratch_shapes=[pltpu.VMEM((k, tn), BF16)],
        compiler_params=_cparams(2),
        name=name,
    )(*args)


def _mm_swiglu(x, w_gate_up):
    m, k = x.shape
    f = w_gate_up.shape[1] // 2
    tm = _tile(m, ROW_TILE, BF16_SUBLANES)
    tn = _tile(f, COL_TILE // 2)
    nj = f // tn
    row = functools.partial(_snake_row, n_rows=m // tm)
    return pl.pallas_call(
        _mm_swiglu_body,
        grid=(nj, m // tm),
        in_specs=[pl.BlockSpec((tm, k), lambda j, i: (row(j, i), 0)),
                  pl.BlockSpec((k, tn), lambda j, i: (0, j)),
                  pl.BlockSpec((k, tn), lambda j, i: (0, nj + j))],
        out_specs=pl.BlockSpec((tm, tn), lambda j, i: (row(j, i), j)),
        out_shape=jax.ShapeDtypeStruct((m, f), BF16),
        scratch_shapes=[pltpu.VMEM((k, tn), BF16), pltpu.VMEM((k, tn), BF16)],
        compiler_params=_cparams(2),
        name="mm_swiglu",
    )(x, w_gate_up, w_gate_up)


def _mm_long_k_residual(x, w, residual):
    k = x.shape[1]
    k_blocks = 1
    while k // k_blocks > MAX_RESIDENT_K or k % k_blocks or (k // k_blocks) % LANES:
        k_blocks += 1
    for b in range(k_blocks):
        residual = _mm(x, w, residual=residual, k_blocks=k_blocks, k_block=b)
    return residual


LOG2E = math.log2(math.e)
SLOPE_TERMS = 3


def _col_reduce(x, op):
    rows = x.shape[0]
    while rows > 8 and rows % 16 == 0:
        rows //= 2
        x = op(x[:rows, :], x[rows:, :])
    red = jnp.max if op is jnp.maximum else jnp.sum
    return red(x, axis=0, keepdims=True)


def _attn_body(qc_ref, q_ref, k_ref, v_ref, o_ref, ka_ref, qa_ref, vt_ref, *, n_blocks, top_k):
    blk = MOBA_BLOCK
    blk_shift = int(math.log2(blk))
    s_len = n_blocks * blk
    hd = q_ref.shape[1]
    nt_dims = (((1,), (1,)), ((), ()))

    km = jnp.concatenate(
        [jnp.mean(k_ref[n * blk:(n + 1) * blk, :].astype(F32), axis=0, keepdims=True)
         for n in range(n_blocks)], axis=0)
    km_hi = km.astype(BF16)
    km_lo = (km - km_hi.astype(F32)).astype(BF16)
    km2 = jnp.concatenate([km_hi, km_lo], axis=1)
    q = q_ref[...]
    q2 = jnp.concatenate([q, q], axis=1)
    gate = lax.dot_general(km2, q2, nt_dims, preferred_element_type=F32)

    n_iota = lax.broadcasted_iota(jnp.int32, (n_blocks, s_len), 0)
    q_blk = lax.shift_right_logical(lax.broadcasted_iota(jnp.int32, (n_blocks, s_len), 1), blk_shift)
    valid = n_iota < q_blk
    g = jnp.where(valid, gate, -jnp.inf)
    rank = jnp.zeros((n_blocks, s_len), jnp.int32)
    for mm in range(n_blocks):
        gm = g[mm:mm + 1, :]
        tie = jnp.where(n_iota > mm, 1, 0)
        rank = rank + jnp.where(gm > g, 1, jnp.where(gm == g, tie, 0))
    sel = jnp.where(valid, jnp.where(rank < top_k, 0.0, NEG),
                    jnp.where(n_iota == q_blk, 0.0, NEG))
    sel_t = jnp.concatenate([sel, jnp.zeros((LANES - n_blocks, s_len), F32)], axis=0).T

    qa_ref[:, :hd] = q
    qa_ref[:, hd:] = (sel_t + qc_ref[...]).astype(BF16)
    t = lax.broadcasted_iota(jnp.int32, (s_len, LANES), 0)
    lane = lax.broadcasted_iota(jnp.int32, (s_len, LANES), 1)
    t_blk = lax.shift_right_logical(t, blk_shift)
    t_off = jnp.bitwise_and(t, blk - 1)
    k_cols = jnp.where(lane < n_blocks, jnp.where(lane == t_blk, 1.0, 0.0),
                       jnp.where(lane < n_blocks + SLOPE_TERMS, t_off.astype(F32),
                                 jnp.where(lane < n_blocks + 2 * SLOPE_TERMS, t_blk.astype(F32), 0.0)))
    ka_ref[:, :hd] = k_ref[...]
    ka_ref[:, hd:] = k_cols.astype(BF16)

    vt_ref[...] = v_ref[...].astype(F32).T.astype(vt_ref.dtype)

    r = lax.broadcasted_iota(jnp.int32, (blk, blk), 1)
    c = lax.broadcasted_iota(jnp.int32, (blk, blk), 0)
    causal = c <= r

    def scores(i):
        return lax.dot_general(ka_ref[:(i + 1) * blk, :], qa_ref[i * blk:(i + 1) * blk, :], nt_dims,
                               preferred_element_type=F32)

    ahead = 2
    s_queue = [scores(i) for i in range(min(ahead, n_blocks))]
    for i in range(n_blocks):
        sl = slice(i * blk, (i + 1) * blk)
        n_keys = (i + 1) * blk
        s = s_queue.pop(0)
        if i + ahead < n_blocks:
            s_queue.append(scores(i + ahead))
        s_own = jnp.where(causal, s[i * blk:, :], NEG)
        m = _col_reduce(s_own, jnp.maximum)
        if i > 0:
            s_past = s[:i * blk, :]
            m = jnp.maximum(m, _col_reduce(s_past, jnp.maximum))
        p_own = jnp.exp2(s_own - m)
        l = _col_reduce(p_own, jnp.add)
        p = p_own.astype(BF16)
        if i > 0:
            p_past = jnp.exp2(s_past - m)
            l = l + _col_reduce(p_past, jnp.add)
            p = jnp.concatenate([p_past.astype(BF16), p], axis=0)
        acc = jnp.dot(vt_ref[:, :n_keys], p, preferred_element_type=F32)
        o_ref[sl, :] = (acc * (1.0 / l)).T.astype(o_ref.dtype)


def _moba_attention(qkv, batch, seq, n_heads, head_dim):
    assert seq % MOBA_BLOCK == 0 and head_dim == LANES
    n_blocks = seq // MOBA_BLOCK
    assert n_blocks + 2 * SLOPE_TERMS <= LANES
    top_k = min(MOBA_TOP_K, n_blocks - 1)
    rest = jnp.asarray(np.power(2.0, -8.0 * np.arange(1, n_heads + 1) / n_heads).astype(np.float32)) * LOG2E
    terms = []
    for _ in range(SLOPE_TERMS):
        term = rest.astype(BF16).astype(F32)
        terms.append(term)
        rest = rest - term
    cols = jnp.stack(terms + [MOBA_BLOCK * term for term in terms], axis=1)
    q_cols = jnp.zeros((n_heads, 1, LANES), F32).at[:, 0, n_blocks:n_blocks + 2 * SLOPE_TERMS].set(cols)
    blk_spec = lambda off: pl.BlockSpec((seq, head_dim), lambda b, h: (b, off + h))
    return pl.pallas_call(
        functools.partial(_attn_body, n_blocks=n_blocks, top_k=top_k),
        grid=(batch, n_heads),
        in_specs=[pl.BlockSpec((None, 1, LANES), lambda b, h: (h, 0, 0)),
                  blk_spec(0), blk_spec(n_heads), blk_spec(2 * n_heads)],
        out_specs=pl.BlockSpec((seq, head_dim), lambda b, h: (b, h)),
        out_shape=jax.ShapeDtypeStruct((batch * seq, n_heads * head_dim), BF16),
        scratch_shapes=[pltpu.VMEM((seq, 2 * head_dim), BF16),
                        pltpu.VMEM((seq, 2 * head_dim), BF16),
                        pltpu.VMEM((head_dim, seq), BF16)],
        compiler_params=_cparams(2),
        name="moba_attention",
    )(q_cols, qkv, qkv, qkv)


def _conv_gate_body(b_ref, c_ref, h_ref, cp_ref, hp_ref, w_ref, o_ref, *, seq):
    tm = c_ref.shape[0]
    u = c_ref[...].astype(F32) * h_ref[...].astype(F32)
    keep = jnp.where((pl.program_id(0) * tm) % seq == 0, 0.0, 1.0)
    up = cp_ref[...].astype(F32) * hp_ref[...].astype(F32) * keep
    last = up.shape[0] - 1
    row = lax.broadcasted_iota(jnp.int32, u.shape, 0)
    u1 = jnp.where(row == 0, up[last:last + 1, :], pltpu.roll(u, 1, 0))
    u2 = jnp.where(row == 0, up[last - 1:last, :],
                   jnp.where(row == 1, up[last:last + 1, :], pltpu.roll(u, 2, 0)))
    y = w_ref[0:1, :] * u2 + w_ref[1:2, :] * u1 + w_ref[2:3, :] * u
    o_ref[...] = (b_ref[...].astype(F32) * y).astype(o_ref.dtype)


def _conv_gate(bch, conv_w, seq):
    n, d3 = bch.shape
    d = d3 // 3
    tm = _tile(seq, 512, BF16_SUBLANES)
    tc = _tile(d, 1024)
    nc = d // tc
    halo = BF16_SUBLANES
    hb = tm // halo
    prev = lambda off: (lambda i, j: (jnp.maximum(i * hb - 1, 0), off + j))
    return pl.pallas_call(
        functools.partial(_conv_gate_body, seq=seq),
        grid=(n // tm, nc),
        in_specs=[pl.BlockSpec((tm, tc), lambda i, j: (i, j)),
                  pl.BlockSpec((tm, tc), lambda i, j: (i, nc + j)),
                  pl.BlockSpec((tm, tc), lambda i, j: (i, 2 * nc + j)),
                  pl.BlockSpec((halo, tc), prev(nc)),
                  pl.BlockSpec((halo, tc), prev(2 * nc)),
                  pl.BlockSpec((CONV_WIDTH, tc), lambda i, j: (0, j))],
        out_specs=pl.BlockSpec((tm, tc), lambda i, j: (i, j)),
        out_shape=jax.ShapeDtypeStruct((n, d), BF16),
        compiler_params=_cparams(2),
        name="conv_gate",
    )(bch, bch, bch, bch, bch, conv_w)


def _router_body(x_ref, g_ref, wr_ref, xs_ref, idx_ref, wgt_ref, *, n_experts):
    x = x_ref[...]
    y = x * lax.rsqrt(jnp.mean(x * x, axis=-1, keepdims=True) + RMS_EPS) * g_ref[...]
    tm, d = y.shape
    for s in range(d // LANES):
        xs_ref[pl.ds(s, tm, stride=SLAB_ROWS), :] = y[:, s * LANES:(s + 1) * LANES]
    for s in range(d // LANES, SLAB_ROWS):
        xs_ref[pl.ds(s, tm, stride=SLAB_ROWS), :] = jnp.zeros((tm, LANES), F32)
    y_hi = y.astype(BF16)
    y_lo = (y - y_hi.astype(F32)).astype(BF16)
    wr = wr_ref[...]
    wr_hi = wr.astype(BF16)
    wr_lo = (wr - wr_hi.astype(F32)).astype(BF16)
    logits = (jnp.dot(y_hi, wr_hi, preferred_element_type=F32)
              + (jnp.dot(y_hi, wr_lo, preferred_element_type=F32)
                 + jnp.dot(y_lo, wr_hi, preferred_element_type=F32)))
    lane = lax.broadcasted_iota(jnp.int32, logits.shape, 1).astype(F32)
    big = float(logits.shape[1])
    lg = jnp.where(lane < n_experts, logits, -jnp.inf)
    m1 = jnp.max(lg, axis=-1, keepdims=True)
    i1 = jnp.min(jnp.where(lg == m1, lane, big), axis=-1, keepdims=True)
    lg2 = jnp.where(lane == i1, -jnp.inf, lg)
    m2 = jnp.max(lg2, axis=-1, keepdims=True)
    i2 = jnp.min(jnp.where(lg2 == m2, lane, big), axis=-1, keepdims=True)
    e = jnp.exp(m2 - m1)
    w1 = 1.0 / (1.0 + e)
    w2 = e * w1
    k = idx_ref.shape[1]
    idx_ref[...] = jnp.where(lane == 0.0, i1, i2)[:, :k].astype(jnp.int32)
    wgt_ref[...] = jnp.where(lane == 0.0, w1, w2)[:, :k]


def _router(h, g, w_router):
    n, d = h.shape
    assert d % LANES == 0 and d // LANES <= SLAB_ROWS
    n_experts = w_router.shape[1]
    tm = _tile(n, NORM_ROW_TILE, 8)
    wr = jnp.pad(w_router, ((0, 0), (0, LANES - n_experts)))
    return pl.pallas_call(
        functools.partial(_router_body, n_experts=n_experts),
        grid=(n // tm,),
        in_specs=[pl.BlockSpec((tm, d), lambda i: (i, 0)),
                  pl.BlockSpec((1, d), lambda i: (0, 0)),
                  pl.BlockSpec((d, LANES), lambda i: (0, 0))],
        out_specs=[pl.BlockSpec((tm * SLAB_ROWS, LANES), lambda i: (i, 0)),
                   pl.BlockSpec((tm, EXPERT_TOP_K), lambda i: (i, 0)),
                   pl.BlockSpec((tm, EXPERT_TOP_K), lambda i: (i, 0))],
        out_shape=[jax.ShapeDtypeStruct((n * SLAB_ROWS, LANES), F32),
                   jax.ShapeDtypeStruct((n, EXPERT_TOP_K), jnp.int32),
                   jax.ShapeDtypeStruct((n, EXPERT_TOP_K), F32)],
        compiler_params=_cparams(1),
        name="router",
    )(h, g.reshape(1, d), wr)


def _group_by_expert(top_idx, n_experts, align):
    n = top_idx.shape[0]
    n_pairs = n * EXPERT_TOP_K
    e_flat = top_idx.reshape(-1)
    onehot = (e_flat[:, None] == jnp.arange(n_experts, dtype=jnp.int32)[None, :]).astype(jnp.int32)
    csum = jnp.cumsum(onehot, axis=0)
    rank = jnp.sum(onehot * csum, axis=1) - 1
    counts = csum[-1]
    padded = (counts + align - 1) // align * align
    ends = jnp.cumsum(padded)
    starts = ends - padded
    valid_starts = ends - counts
    dest = (jnp.sum(onehot * valid_starts[None, :], axis=1) + rank).astype(jnp.int32)
    p_rows = n_pairs + n_experts * align
    row_src = jnp.zeros((p_rows,), jnp.int32).at[dest].set(
        jnp.arange(n_pairs, dtype=jnp.int32) // EXPERT_TOP_K)
    return dest, row_src, (valid_starts, ends)


def _tile_table(groups, p_rows, tm):
    valid_starts, ends = groups
    n_experts = ends.shape[0]
    tile_start = jnp.arange(p_rows // tm, dtype=jnp.int32) * tm
    before = jnp.sum((tile_start[:, None] >= ends[None, :]).astype(jnp.int32), axis=1)
    live = before < n_experts
    n_live = ends[-1] // tm
    expert = jnp.minimum(before, n_experts - 1)
    rows = jnp.where(live, jnp.clip(tile_start + tm - valid_starts[expert], 0, tm), 0)
    expert = jnp.where(live, expert, expert[jnp.maximum(n_live - 1, 0)])
    return expert.astype(jnp.int32), rows.astype(jnp.int32), n_live.reshape(1).astype(jnp.int32)


def _gather_rows_body(src_ref, x_hbm, o_ref, buf0_ref, buf1_ref, sems):
    tg, d = o_ref.shape
    n_slab = d // LANES
    i = pl.program_id(0)
    bufs = (buf0_ref, buf1_ref)

    def start_tile(tile, slot):
        def start(g, carry):
            for u in range(DMA_ISSUE_UNROLL):
                r = g * DMA_ISSUE_UNROLL + u
                src = pl.multiple_of(src_ref[tile * tg + r] * SLAB_ROWS, SLAB_ROWS)
                dst = pl.multiple_of(r * SLAB_ROWS, SLAB_ROWS)
                pltpu.make_async_copy(x_hbm.at[pl.ds(src, n_slab), :],
                                      bufs[slot].at[pl.ds(dst, n_slab), :],
                                      sems.at[slot]).start(priority=u % 2)
            return carry

        lax.fori_loop(0, tg // DMA_ISSUE_UNROLL, start, 0)

    def finish_tile(slot):
        pltpu.make_async_copy(x_hbm.at[pl.ds(0, tg * n_slab), :],
                              bufs[slot].at[pl.ds(0, tg * n_slab), :], sems.at[slot]).wait()
        for s in range(n_slab):
            o_ref[:, s * LANES:(s + 1) * LANES] = (
                bufs[slot][pl.ds(s, tg, stride=SLAB_ROWS), :].astype(o_ref.dtype))

    @pl.when(i == 0)
    def _():
        start_tile(0, 0)

    for slot in (0, 1):
        @pl.when(jnp.logical_and(i % 2 == slot, i + 1 < pl.num_programs(0)))
        def _(slot=slot):
            start_tile(i + 1, 1 - slot)

        @pl.when(i % 2 == slot)
        def _(slot=slot):
            finish_tile(slot)


def _gather_rows(xs, row_src, d):
    p_rows = row_src.shape[0]
    tg = _tile(p_rows, GATHER_ROW_TILE, BF16_SUBLANES)
    return pl.pallas_call(
        _gather_rows_body,
        grid_spec=pltpu.PrefetchScalarGridSpec(
            num_scalar_prefetch=1,
            grid=(p_rows // tg,),
            in_specs=[pl.BlockSpec(memory_space=pl.ANY)],
            out_specs=pl.BlockSpec((tg, d), lambda i, src: (i, 0)),
            scratch_shapes=[pltpu.VMEM((tg * SLAB_ROWS, LANES), F32),
                            pltpu.VMEM((tg * SLAB_ROWS, LANES), F32),
                            pltpu.SemaphoreType.DMA((2,))]),
        out_shape=jax.ShapeDtypeStruct((p_rows, d), BF16),
        compiler_params=_cparams(1),
        name="moe_gather",
    )(row_src, xs)


def _first_tile_of_expert(te_ref, i):
    return jnp.logical_or(i == 0, te_ref[i] != te_ref[jnp.maximum(i - 1, 0)])


def _for_valid_rows(rows, o_ref, compute):
    tm = o_ref.shape[0]
    sub = min(MOE_SUB_ROWS, tm)
    for m in range(sub, tm + 1, sub):
        @pl.when(jnp.logical_and(rows > m - sub, rows <= m))
        def _(m=m):
            o_ref[tm - m:, :] = compute(tm - m)
            if m < tm:
                o_ref[:tm - m, :] = jnp.zeros((tm - m, o_ref.shape[1]), o_ref.dtype)

    @pl.when(rows <= 0)
    def _():
        o_ref[...] = jnp.zeros(o_ref.shape, o_ref.dtype)


def _moe_swiglu_body(te_ref, tr_ref, nl_ref, x_ref, wg_ref, wu_ref, o_ref, wgb_ref, wub_ref):
    i = pl.program_id(1)

    def emit(dot):
        def compute(row0):
            x = x_ref[row0:, :]
            g = dot(x, wg_ref, wgb_ref)
            u = dot(x, wu_ref, wub_ref)
            return _swiglu_act(g, u).astype(o_ref.dtype)

        _for_valid_rows(tr_ref[i], o_ref, compute)

    _with_weights(_first_tile_of_expert(te_ref, i), emit)


def _moe_down_body(te_ref, tr_ref, nl_ref, x_ref, w_ref, o_ref, wb_ref):
    i = pl.program_id(1)

    def emit(dot):
        _for_valid_rows(tr_ref[i], o_ref, lambda row0: dot(x_ref[row0:, :], w_ref, wb_ref))

    _with_weights(_first_tile_of_expert(te_ref, i), emit)


def _live_tile(i, nl):
    return jnp.minimum(i, nl[0] - 1)


def _moe_swiglu(xg, w_gate_up, groups, tm):
    p_rows, k = xg.shape
    f = w_gate_up.shape[2] // 2
    tn = _tile(f, COL_TILE)
    nj = f // tn
    tile_expert, tile_rows, n_live = _tile_table(groups, p_rows, tm)
    return pl.pallas_call(
        _moe_swiglu_body,
        grid_spec=pltpu.PrefetchScalarGridSpec(
            num_scalar_prefetch=3,
            grid=(nj, p_rows // tm),
            in_specs=[pl.BlockSpec((tm, k), lambda j, i, te, tr, nl: (_live_tile(i, nl), 0)),
                      pl.BlockSpec((None, k, tn), lambda j, i, te, tr, nl: (te[i], 0, j)),
                      pl.BlockSpec((None, k, tn), lambda j, i, te, tr, nl: (te[i], 0, nj + j))],
            out_specs=pl.BlockSpec((tm, tn), lambda j, i, te, tr, nl: (i, j)),
            scratch_shapes=[pltpu.VMEM((k, tn), BF16), pltpu.VMEM((k, tn), BF16)]),
        out_shape=jax.ShapeDtypeStruct((p_rows, f), BF16),
        compiler_params=_cparams(2),
        name="moe_swiglu",
    )(tile_expert, tile_rows, n_live, xg, w_gate_up, w_gate_up)


def _moe_down(act, w_down, groups, tm):
    p_rows, k = act.shape
    n = w_down.shape[2]
    tn = _tile(n, COL_TILE)
    tile_expert, tile_rows, n_live = _tile_table(groups, p_rows, tm)
    return pl.pallas_call(
        _moe_down_body,
        grid_spec=pltpu.PrefetchScalarGridSpec(
            num_scalar_prefetch=3,
            grid=(n // tn, p_rows // tm),
            in_specs=[pl.BlockSpec((tm, k), lambda j, i, te, tr, nl: (_live_tile(i, nl), 0)),
                      pl.BlockSpec((None, k, tn), lambda j, i, te, tr, nl: (te[i], 0, j))],
            out_specs=pl.BlockSpec((tm, tn), lambda j, i, te, tr, nl: (i, j)),
            scratch_shapes=[pltpu.VMEM((k, tn), BF16)]),
        out_shape=jax.ShapeDtypeStruct((p_rows, n), F32),
        compiler_params=_cparams(2),
        name="moe_down",
    )(tile_expert, tile_rows, n_live, act, w_down)


def _combine_body(dest_ref, h_ref, w_ref, y_hbm, o_ref, ya0_ref, yb0_ref, ya1_ref, yb1_ref, sems):
    tc = h_ref.shape[0]
    i = pl.program_id(0)
    bufs = ((ya0_ref, yb0_ref), (ya1_ref, yb1_ref))

    def start_tile(tile, slot):
        def start(g, carry):
            for u in range(DMA_ISSUE_UNROLL):
                r = g * DMA_ISSUE_UNROLL + u
                pair = (tile * tc + r) * EXPERT_TOP_K
                for c in range(EXPERT_TOP_K):
                    pltpu.make_async_copy(y_hbm.at[pl.ds(dest_ref[pair + c], 1), :],
                                          bufs[slot][c].at[pl.ds(r, 1), :],
                                          sems.at[slot, c]).start(priority=c % 2)
            return carry

        lax.fori_loop(0, tc // DMA_ISSUE_UNROLL, start, 0)

    def finish_tile(slot):
        for c in range(EXPERT_TOP_K):
            pltpu.make_async_copy(y_hbm.at[pl.ds(0, tc), :], bufs[slot][c], sems.at[slot, c]).wait()
        w = w_ref[...]
        o_ref[...] = h_ref[...] + (w[:, 0:1] * bufs[slot][0][...] + w[:, 1:2] * bufs[slot][1][...])

    @pl.when(i == 0)
    def _():
        start_tile(0, 0)

    for slot in (0, 1):
        @pl.when(jnp.logical_and(i % 2 == slot, i + 1 < pl.num_programs(0)))
        def _(slot=slot):
            start_tile(i + 1, 1 - slot)

        @pl.when(i % 2 == slot)
        def _(slot=slot):
            finish_tile(slot)


def _combine(h, top_w, y, dest):
    n, d = h.shape
    tc = _tile(n, GATHER_ROW_TILE, 8)
    return pl.pallas_call(
        _combine_body,
        grid_spec=pltpu.PrefetchScalarGridSpec(
            num_scalar_prefetch=1,
            grid=(n // tc,),
            in_specs=[pl.BlockSpec((tc, d), lambda i, dst: (i, 0)),
                      pl.BlockSpec((tc, EXPERT_TOP_K), lambda i, dst: (i, 0)),
                      pl.BlockSpec(memory_space=pl.ANY)],
            out_specs=pl.BlockSpec((tc, d), lambda i, dst: (i, 0)),
            scratch_shapes=[pltpu.VMEM((tc, d), F32) for _ in range(2 * EXPERT_TOP_K)]
            + [pltpu.SemaphoreType.DMA((2, EXPERT_TOP_K))]),
        out_shape=jax.ShapeDtypeStruct((n, d), F32),
        compiler_params=_cparams(1),
        name="moe_combine",
    )(dest, h, top_w, y)


def kernel(x, attn_norm, w_qkv, q_norm, k_norm, w_attn_out, ffn_norm, w_ffn_gate_up, w_ffn_down,
           conv_norm, w_conv_in, conv_w, w_conv_out, moe_norm, w_router, w_expert_gate_up,
           w_expert_down):
    batch, seq, d = x.shape
    head_dim = q_norm.shape[-1]
    n_heads = d // head_dim
    n_experts = w_router.shape[-1]
    assert attn_norm.shape[0] == 1 and conv_norm.shape[0] == 1, "two-layer trunk: attention, then conv"
    n_tok = batch * seq
    h = x.reshape(n_tok, d)

    gain = jnp.concatenate([jnp.tile(q_norm[0] * (LOG2E / math.sqrt(head_dim)), n_heads),
                            jnp.tile(k_norm[0], n_heads),
                            jnp.ones((d,), F32)]).reshape(1, 3 * d)
    xn = _rmsnorm(h, attn_norm[0])
    qkv = _mm(xn, w_qkv[0], qk_gain=gain, n_norm_cols=2 * d, head_dim=head_dim)
    o = _moba_attention(qkv, batch, seq, n_heads, head_dim)
    h = _mm(o, w_attn_out[0], residual=h)
    xn = _rmsnorm(h, ffn_norm[0])
    act = _mm_swiglu(xn, w_ffn_gate_up[0])
    h = _mm_long_k_residual(act, w_ffn_down[0], h)

    xn = _rmsnorm(h, conv_norm[0])
    bch = _mm(xn, w_conv_in[0])
    z = _conv_gate(bch, conv_w[0], seq)
    h = _mm(z, w_conv_out[0], residual=h)
    xs, top_idx, top_w = _router(h, moe_norm[0], w_router[0])
    tm = _tile(n_tok, MOE_ROW_TILE, BF16_SUBLANES)
    dest, row_src, groups = _group_by_expert(top_idx, n_experts, tm)
    xg = _gather_rows(xs, row_src, d)
    act = _moe_swiglu(xg, w_expert_gate_up[0], groups, tm)
    y = _moe_down(act, w_expert_down[0], groups, tm)
    h = _combine(h, top_w, y, dest)
    return h.reshape(batch, seq, d)
```

```python
import functools
import math

import numpy as np
import jax
import jax.numpy as jnp
from jax import lax
from jax.experimental import pallas as pl
from jax.experimental.pallas import tpu as pltpu

F32 = jnp.float32
BF16 = jnp.bfloat16

MOBA_BLOCK = 256
MOBA_TOP_K = 3
EXPERT_TOP_K = 2
CONV_WIDTH = 3
RMS_EPS = 1e-6
NEG = -1e30

LANES = 128
BF16_SUBLANES = 16
V7X_VMEM_BYTES = 64 * 1024 * 1024
VMEM_LIMIT_BYTES = V7X_VMEM_BYTES - 8 * 1024 * 1024

ROW_TILE = 1024
COL_TILE = 512
MAX_RESIDENT_K = 4096
MOE_ROW_TILE = 512
MOE_SUB_ROWS = 256
NORM_ROW_TILE = 256
GATHER_ROW_TILE = 256
DMA_ISSUE_UNROLL = 8
SLAB_ROWS = 40
FRESH_K_SLAB = 512


def _tile(dim, pref, mult=LANES):
    if dim <= pref:
        return dim
    t = pref - pref % mult
    while t >= mult:
        if dim % t == 0:
            return t
        t -= mult
    return dim


def _cparams(n_axes):
    return pltpu.CompilerParams(dimension_semantics=("arbitrary",) * n_axes,
                                vmem_limit_bytes=VMEM_LIMIT_BYTES)


def _dot_fresh(x, w_ref, wb_ref):
    k = w_ref.shape[0]
    kc = _tile(k, FRESH_K_SLAB, BF16_SUBLANES)
    acc = None
    for c in range(k // kc):
        sl = slice(c * kc, (c + 1) * kc)
        wb_ref[sl, :] = w_ref[sl, :].astype(wb_ref.dtype)
        part = jnp.dot(x[:, sl], wb_ref[sl, :], preferred_element_type=F32)
        acc = part if acc is None else acc + part
    return acc


def _dot_converted(x, w_ref, wb_ref):
    return jnp.dot(x, wb_ref[...], preferred_element_type=F32)


def _with_weights(fresh, emit):
    @pl.when(fresh)
    def _():
        emit(_dot_fresh)

    @pl.when(jnp.logical_not(fresh))
    def _():
        emit(_dot_converted)


def _rmsnorm_body(x_ref, g_ref, o_ref):
    x = x_ref[...]
    inv = lax.rsqrt(jnp.mean(x * x, axis=-1, keepdims=True) + RMS_EPS)
    o_ref[...] = (x * inv * g_ref[...]).astype(o_ref.dtype)


def _rmsnorm(x, g):
    n, d = x.shape
    tm = _tile(n, NORM_ROW_TILE, BF16_SUBLANES)
    return pl.pallas_call(
        _rmsnorm_body,
        grid=(n // tm,),
        in_specs=[pl.BlockSpec((tm, d), lambda i: (i, 0)),
                  pl.BlockSpec((1, d), lambda i: (0, 0))],
        out_specs=pl.BlockSpec((tm, d), lambda i: (i, 0)),
        out_shape=jax.ShapeDtypeStruct((n, d), BF16),
        compiler_params=_cparams(1),
        name="rmsnorm",
    )(x, g.reshape(1, d))


def _snake_row(j, i, n_rows):
    return i + (j % 2) * (n_rows - 1 - 2 * i)


def _mm_plain_body(x_ref, w_ref, o_ref, wb_ref):
    def emit(dot):
        o_ref[...] = dot(x_ref[...], w_ref, wb_ref).astype(o_ref.dtype)

    _with_weights(pl.program_id(1) == 0, emit)


def _mm_residual_body(x_ref, w_ref, r_ref, o_ref, wb_ref):
    def emit(dot):
        o_ref[...] = r_ref[...] + dot(x_ref[...], w_ref, wb_ref)

    _with_weights(pl.program_id(1) == 0, emit)


def _mm_qknorm_body(x_ref, w_ref, g_ref, o_ref, wb_ref, *, n_norm_tiles, head_dim):
    j = pl.program_id(0)

    def emit(dot):
        acc = dot(x_ref[...], w_ref, wb_ref)

        @pl.when(j < n_norm_tiles)
        def _():
            for hh in range(acc.shape[1] // head_dim):
                sl = slice(hh * head_dim, (hh + 1) * head_dim)
                a = acc[:, sl]
                inv = lax.rsqrt(jnp.mean(a * a, axis=-1, keepdims=True) + RMS_EPS)
                o_ref[:, sl] = (a * inv * g_ref[:, sl]).astype(o_ref.dtype)

        @pl.when(j >= n_norm_tiles)
        def _():
            o_ref[...] = acc.astype(o_ref.dtype)

    _with_weights(pl.program_id(1) == 0, emit)


def _swiglu_act(g, u):
    return g * (1.0 / (1.0 + jnp.exp(-g))) * u


def _mm_swiglu_body(x_ref, wg_ref, wu_ref, o_ref, wgb_ref, wub_ref):
    def emit(dot):
        x = x_ref[...]
        g = dot(x, wg_ref, wgb_ref)
        u = dot(x, wu_ref, wub_ref)
        o_ref[...] = _swiglu_act(g, u).astype(o_ref.dtype)

    _with_weights(pl.program_id(1) == 0, emit)


def _mm(x, w, *, out_dtype=BF16, residual=None, qk_gain=None, n_norm_cols=0, head_dim=LANES,
        k_blocks=1, k_block=0):
    m = x.shape[0]
    k = x.shape[1] // k_blocks
    n = w.shape[1]
    tm = _tile(m, ROW_TILE, BF16_SUBLANES)
    tn = _tile(n, COL_TILE)
    grid = (n // tn, m // tm)
    row = functools.partial(_snake_row, n_rows=m // tm)
    in_specs = [pl.BlockSpec((tm, k), lambda j, i: (row(j, i), k_block)),
                pl.BlockSpec((k, tn), lambda j, i: (k_block, j))]
    args = [x, w]
    if residual is not None:
        body = _mm_residual_body
        in_specs.append(pl.BlockSpec((tm, tn), lambda j, i: (row(j, i), j)))
        args.append(residual)
        out_dtype = F32
        name = "mm_residual"
    elif qk_gain is not None:
        assert n_norm_cols % tn == 0 and tn % head_dim == 0
        body = functools.partial(_mm_qknorm_body, n_norm_tiles=n_norm_cols // tn, head_dim=head_dim)
        in_specs.append(pl.BlockSpec((1, tn), lambda j, i: (0, j)))
        args.append(qk_gain)
        name = "mm_qknorm"
    else:
        body = _mm_plain_body
        name = "mm_plain"
    return pl.pallas_call(
        body,
        grid=grid,
        in_specs=in_specs,
        out_specs=pl.BlockSpec((tm, tn), lambda j, i: (row(j, i), j)),
        out_shape=jax.ShapeDtypeStruct((m, n), out_dtype),
        scratch_shapes=[pltpu.VMEM((k, tn), BF16)],
        compiler_params=_cparams(2),
        name=name,
    )(*args)


def _mm_swiglu(x, w_gate_up):
    m, k = x.shape
    f = w_gate_up.shape[1] // 2
    tm = _tile(m, ROW_TILE // 2, BF16_SUBLANES)
    tn = _tile(f, COL_TILE)
    nj = f // tn
    row = functools.partial(_snake_row, n_rows=m // tm)
    return pl.pallas_call(
        _mm_swiglu_body,
        grid=(nj, m // tm),
        in_specs=[pl.BlockSpec((tm, k), lambda j, i: (row(j, i), 0)),
                  pl.BlockSpec((k, tn), lambda j, i: (0, j)),
                  pl.BlockSpec((k, tn), lambda j, i: (0, nj + j))],
        out_specs=pl.BlockSpec((tm, tn), lambda j, i: (row(j, i), j)),
        out_shape=jax.ShapeDtypeStruct((m, f), BF16),
        scratch_shapes=[pltpu.VMEM((k, tn), BF16), pltpu.VMEM((k, tn), BF16)],
        compiler_params=_cparams(2),
        name="mm_swiglu",
    )(x, w_gate_up, w_gate_up)


def _mm_long_k_residual(x, w, residual):
    k = x.shape[1]
    k_blocks = 1
    while k // k_blocks > MAX_RESIDENT_K or k % k_blocks or (k // k_blocks) % LANES:
        k_blocks += 1
    for b in range(k_blocks):
        residual = _mm(x, w, residual=residual, k_blocks=k_blocks, k_block=b)
    return residual


LOG2E = math.log2(math.e)
SLOPE_TERMS = 3


def _col_reduce(x, op):
    rows = x.shape[0]
    while rows > 8 and rows % 16 == 0:
        rows //= 2
        x = op(x[:rows, :], x[rows:, :])
    red = jnp.max if op is jnp.maximum else jnp.sum
    return red(x, axis=0, keepdims=True)


def _attn_body(qc_ref, q_ref, k_ref, v_ref, o_ref, ka_ref, qa_ref, vt_ref, *, n_blocks, top_k):
    blk = MOBA_BLOCK
    blk_shift = int(math.log2(blk))
    s_len = n_blocks * blk
    hd = q_ref.shape[1]
    nt_dims = (((1,), (1,)), ((), ()))

    km = jnp.concatenate(
        [jnp.mean(k_ref[n * blk:(n + 1) * blk, :].astype(F32), axis=0, keepdims=True)
         for n in range(n_blocks)], axis=0)
    km_hi = km.astype(BF16)
    km_lo = (km - km_hi.astype(F32)).astype(BF16)
    km2 = jnp.concatenate([km_hi, km_lo], axis=1)
    q = q_ref[...]
    q2 = jnp.concatenate([q, q], axis=1)
    gate = lax.dot_general(km2, q2, nt_dims, preferred_element_type=F32)

    n_iota = lax.broadcasted_iota(jnp.int32, (n_blocks, s_len), 0)
    q_blk = lax.shift_right_logical(lax.broadcasted_iota(jnp.int32, (n_blocks, s_len), 1), blk_shift)
    valid = n_iota < q_blk
    g = jnp.where(valid, gate, -jnp.inf)
    rank = jnp.zeros((n_blocks, s_len), jnp.int32)
    for mm in range(n_blocks):
        gm = g[mm:mm + 1, :]
        tie = jnp.where(n_iota > mm, 1, 0)
        rank = rank + jnp.where(gm > g, 1, jnp.where(gm == g, tie, 0))
    sel = jnp.where(valid, jnp.where(rank < top_k, 0.0, NEG),
                    jnp.where(n_iota == q_blk, 0.0, NEG))
    sel_t = jnp.concatenate([sel, jnp.zeros((LANES - n_blocks, s_len), F32)], axis=0).T

    qa_ref[:, :hd] = q
    qa_ref[:, hd:] = (sel_t + qc_ref[...]).astype(BF16)
    t = lax.broadcasted_iota(jnp.int32, (s_len, LANES), 0)
    lane = lax.broadcasted_iota(jnp.int32, (s_len, LANES), 1)
    t_blk = lax.shift_right_logical(t, blk_shift)
    t_off = jnp.bitwise_and(t, blk - 1)
    k_cols = jnp.where(lane < n_blocks, jnp.where(lane == t_blk, 1.0, 0.0),
                       jnp.where(lane < n_blocks + SLOPE_TERMS, t_off.astype(F32),
                                 jnp.where(lane < n_blocks + 2 * SLOPE_TERMS, t_blk.astype(F32), 0.0)))
    ka_ref[:, :hd] = k_ref[...]
    ka_ref[:, hd:] = k_cols.astype(BF16)

    vt_ref[...] = v_ref[...].astype(F32).T.astype(vt_ref.dtype)

    r = lax.broadcasted_iota(jnp.int32, (blk, blk), 1)
    c = lax.broadcasted_iota(jnp.int32, (blk, blk), 0)
    causal = c <= r

    def scores(i):
        return lax.dot_general(ka_ref[:(i + 1) * blk, :], qa_ref[i * blk:(i + 1) * blk, :], nt_dims,
                               preferred_element_type=F32)

    ahead = 2
    s_queue = [scores(i) for i in range(min(ahead, n_blocks))]
    for i in range(n_blocks):
        sl = slice(i * blk, (i + 1) * blk)
        n_keys = (i + 1) * blk
        s = s_queue.pop(0)
        if i + ahead < n_blocks:
            s_queue.append(scores(i + ahead))
        s_own = jnp.where(causal, s[i * blk:, :], NEG)
        m = _col_reduce(s_own, jnp.maximum)
        if i > 0:
            s_past = s[:i * blk, :]
            m = jnp.maximum(m, _col_reduce(s_past, jnp.maximum))
        p_own = jnp.exp2(s_own - m)
        l = _col_reduce(p_own, jnp.add)
        p = p_own.astype(BF16)
        if i > 0:
            p_past = jnp.exp2(s_past - m)
            l = l + _col_reduce(p_past, jnp.add)
            p = jnp.concatenate([p_past.astype(BF16), p], axis=0)
        acc = jnp.dot(vt_ref[:, :n_keys], p, preferred_element_type=F32)
        o_ref[sl, :] = (acc * (1.0 / l)).T.astype(o_ref.dtype)


def _moba_attention(qkv, batch, seq, n_heads, head_dim):
    assert seq % MOBA_BLOCK == 0 and head_dim == LANES
    n_blocks = seq // MOBA_BLOCK
    assert n_blocks + 2 * SLOPE_TERMS <= LANES
    top_k = min(MOBA_TOP_K, n_blocks - 1)
    rest = jnp.asarray(np.power(2.0, -8.0 * np.arange(1, n_heads + 1) / n_heads).astype(np.float32)) * LOG2E
    terms = []
    for _ in range(SLOPE_TERMS):
        term = rest.astype(BF16).astype(F32)
        terms.append(term)
        rest = rest - term
    cols = jnp.stack(terms + [MOBA_BLOCK * term for term in terms], axis=1)
    q_cols = jnp.zeros((n_heads, 1, LANES), F32).at[:, 0, n_blocks:n_blocks + 2 * SLOPE_TERMS].set(cols)
    blk_spec = lambda off: pl.BlockSpec((seq, head_dim), lambda b, h: (b, off + h))
    return pl.pallas_call(
        functools.partial(_attn_body, n_blocks=n_blocks, top_k=top_k),
        grid=(batch, n_heads),
        in_specs=[pl.BlockSpec((None, 1, LANES), lambda b, h: (h, 0, 0)),
                  blk_spec(0), blk_spec(n_heads), blk_spec(2 * n_heads)],
        out_specs=pl.BlockSpec((seq, head_dim), lambda b, h: (b, h)),
        out_shape=jax.ShapeDtypeStruct((batch * seq, n_heads * head_dim), BF16),
        scratch_shapes=[pltpu.VMEM((seq, 2 * head_dim), BF16),
                        pltpu.VMEM((seq, 2 * head_dim), BF16),
                        pltpu.VMEM((head_dim, seq), BF16)],
        compiler_params=_cparams(2),
        name="moba_attention",
    )(q_cols, qkv, qkv, qkv)


def _conv_gate_body(b_ref, c_ref, h_ref, cp_ref, hp_ref, w_ref, o_ref, *, seq):
    tm = c_ref.shape[0]
    u = c_ref[...].astype(F32) * h_ref[...].astype(F32)
    keep = jnp.where((pl.program_id(0) * tm) % seq == 0, 0.0, 1.0)
    up = cp_ref[...].astype(F32) * hp_ref[...].astype(F32) * keep
    last = up.shape[0] - 1
    row = lax.broadcasted_iota(jnp.int32, u.shape, 0)
    u1 = jnp.where(row == 0, up[last:last + 1, :], pltpu.roll(u, 1, 0))
    u2 = jnp.where(row == 0, up[last - 1:last, :],
                   jnp.where(row == 1, up[last:last + 1, :], pltpu.roll(u, 2, 0)))
    y = w_ref[0:1, :] * u2 + w_ref[1:2, :] * u1 + w_ref[2:3, :] * u
    o_ref[...] = (b_ref[...].astype(F32) * y).astype(o_ref.dtype)


def _conv_gate(bch, conv_w, seq):
    n, d3 = bch.shape
    d = d3 // 3
    tm = _tile(seq, 512, BF16_SUBLANES)
    tc = _tile(d, 1024)
    nc = d // tc
    halo = BF16_SUBLANES
    hb = tm // halo
    prev = lambda off: (lambda i, j: (jnp.maximum(i * hb - 1, 0), off + j))
    return pl.pallas_call(
        functools.partial(_conv_gate_body, seq=seq),
        grid=(n // tm, nc),
        in_specs=[pl.BlockSpec((tm, tc), lambda i, j: (i, j)),
                  pl.BlockSpec((tm, tc), lambda i, j: (i, nc + j)),
                  pl.BlockSpec((tm, tc), lambda i, j: (i, 2 * nc + j)),
                  pl.BlockSpec((halo, tc), prev(nc)),
                  pl.BlockSpec((halo, tc), prev(2 * nc)),
                  pl.BlockSpec((CONV_WIDTH, tc), lambda i, j: (0, j))],
        out_specs=pl.BlockSpec((tm, tc), lambda i, j: (i, j)),
        out_shape=jax.ShapeDtypeStruct((n, d), BF16),
        compiler_params=_cparams(2),
        name="conv_gate",
    )(bch, bch, bch, bch, bch, conv_w)


def _router_body(x_ref, g_ref, wr_ref, xs_ref, idx_ref, wgt_ref, *, n_experts):
    x = x_ref[...]
    y = x * lax.rsqrt(jnp.mean(x * x, axis=-1, keepdims=True) + RMS_EPS) * g_ref[...]
    tm, d = y.shape
    for s in range(d // LANES):
        xs_ref[pl.ds(s, tm, stride=SLAB_ROWS), :] = y[:, s * LANES:(s + 1) * LANES]
    for s in range(d // LANES, SLAB_ROWS):
        xs_ref[pl.ds(s, tm, stride=SLAB_ROWS), :] = jnp.zeros((tm, LANES), F32)
    y_hi = y.astype(BF16)
    y_lo = (y - y_hi.astype(F32)).astype(BF16)
    wr = wr_ref[...]
    wr_hi = wr.astype(BF16)
    wr_lo = (wr - wr_hi.astype(F32)).astype(BF16)
    logits = (jnp.dot(y_hi, wr_hi, preferred_element_type=F32)
              + (jnp.dot(y_hi, wr_lo, preferred_element_type=F32)
                 + jnp.dot(y_lo, wr_hi, preferred_element_type=F32)))
    lane = lax.broadcasted_iota(jnp.int32, logits.shape, 1).astype(F32)
    big = float(logits.shape[1])
    lg = jnp.where(lane < n_experts, logits, -jnp.inf)
    m1 = jnp.max(lg, axis=-1, keepdims=True)
    i1 = jnp.min(jnp.where(lg == m1, lane, big), axis=-1, keepdims=True)
    lg2 = jnp.where(lane == i1, -jnp.inf, lg)
    m2 = jnp.max(lg2, axis=-1, keepdims=True)
    i2 = jnp.min(jnp.where(lg2 == m2, lane, big), axis=-1, keepdims=True)
    e = jnp.exp(m2 - m1)
    w1 = 1.0 / (1.0 + e)
    w2 = e * w1
    k = idx_ref.shape[1]
    idx_ref[...] = jnp.where(lane == 0.0, i1, i2)[:, :k].astype(jnp.int32)
    wgt_ref[...] = jnp.where(lane == 0.0, w1, w2)[:, :k]


def _router(h, g, w_router):
    n, d = h.shape
    assert d % LANES == 0 and d // LANES <= SLAB_ROWS
    n_experts = w_router.shape[1]
    tm = _tile(n, NORM_ROW_TILE, 8)
    wr = jnp.pad(w_router, ((0, 0), (0, LANES - n_experts)))
    return pl.pallas_call(
        functools.partial(_router_body, n_experts=n_experts),
        grid=(n // tm,),
        in_specs=[pl.BlockSpec((tm, d), lambda i: (i, 0)),
                  pl.BlockSpec((1, d), lambda i: (0, 0)),
                  pl.BlockSpec((d, LANES), lambda i: (0, 0))],
        out_specs=[pl.BlockSpec((tm * SLAB_ROWS, LANES), lambda i: (i, 0)),
                   pl.BlockSpec((tm, EXPERT_TOP_K), lambda i: (i, 0)),
                   pl.BlockSpec((tm, EXPERT_TOP_K), lambda i: (i, 0))],
        out_shape=[jax.ShapeDtypeStruct((n * SLAB_ROWS, LANES), F32),
                   jax.ShapeDtypeStruct((n, EXPERT_TOP_K), jnp.int32),
                   jax.ShapeDtypeStruct((n, EXPERT_TOP_K), F32)],
        compiler_params=_cparams(1),
        name="router",
    )(h, g.reshape(1, d), wr)


def _group_by_expert(top_idx, n_experts, align):
    n = top_idx.shape[0]
    n_pairs = n * EXPERT_TOP_K
    e_flat = top_idx.reshape(-1)
    onehot = (e_flat[:, None] == jnp.arange(n_experts, dtype=jnp.int32)[None, :]).astype(jnp.int32)
    csum = jnp.cumsum(onehot, axis=0)
    rank = jnp.sum(onehot * csum, axis=1) - 1
    counts = csum[-1]
    padded = (counts + align - 1) // align * align
    ends = jnp.cumsum(padded)
    starts = ends - padded
    valid_starts = ends - counts
    dest = (jnp.sum(onehot * valid_starts[None, :], axis=1) + rank).astype(jnp.int32)
    p_rows = n_pairs + n_experts * align
    row_src = jnp.zeros((p_rows,), jnp.int32).at[dest].set(
        jnp.arange(n_pairs, dtype=jnp.int32) // EXPERT_TOP_K)
    return dest, row_src, (valid_starts, ends)


def _tile_table(groups, p_rows, tm):
    valid_starts, ends = groups
    n_experts = ends.shape[0]
    tile_start = jnp.arange(p_rows // tm, dtype=jnp.int32) * tm
    before = jnp.sum((tile_start[:, None] >= ends[None, :]).astype(jnp.int32), axis=1)
    live = before < n_experts
    n_live = ends[-1] // tm
    expert = jnp.minimum(before, n_experts - 1)
    rows = jnp.where(live, jnp.clip(tile_start + tm - valid_starts[expert], 0, tm), 0)
    expert = jnp.where(live, expert, expert[jnp.maximum(n_live - 1, 0)])
    return expert.astype(jnp.int32), rows.astype(jnp.int32), n_live.reshape(1).astype(jnp.int32)


def _gather_rows_body(src_ref, x_hbm, o_ref, buf0_ref, buf1_ref, sems):
    tg, d = o_ref.shape
    n_slab = d // LANES
    i = pl.program_id(0)
    bufs = (buf0_ref, buf1_ref)

    def start_tile(tile, slot):
        def start(g, carry):
            for u in range(DMA_ISSUE_UNROLL):
                r = g * DMA_ISSUE_UNROLL + u
                src = pl.multiple_of(src_ref[tile * tg + r] * SLAB_ROWS, SLAB_ROWS)
                dst = pl.multiple_of(r * SLAB_ROWS, SLAB_ROWS)
                pltpu.make_async_copy(x_hbm.at[pl.ds(src, n_slab), :],
                                      bufs[slot].at[pl.ds(dst, n_slab), :],
                                      sems.at[slot]).start(priority=u % 2)
            return carry

        lax.fori_loop(0, tg // DMA_ISSUE_UNROLL, start, 0)

    def finish_tile(slot):
        pltpu.make_async_copy(x_hbm.at[pl.ds(0, tg * n_slab), :],
                              bufs[slot].at[pl.ds(0, tg * n_slab), :], sems.at[slot]).wait()
        for s in range(n_slab):
            o_ref[:, s * LANES:(s + 1) * LANES] = (
                bufs[slot][pl.ds(s, tg, stride=SLAB_ROWS), :].astype(o_ref.dtype))

    @pl.when(i == 0)
    def _():
        start_tile(0, 0)

    for slot in (0, 1):
        @pl.when(jnp.logical_and(i % 2 == slot, i + 1 < pl.num_programs(0)))
        def _(slot=slot):
            start_tile(i + 1, 1 - slot)

        @pl.when(i % 2 == slot)
        def _(slot=slot):
            finish_tile(slot)


def _gather_rows(xs, row_src, d):
    p_rows = row_src.shape[0]
    tg = _tile(p_rows, GATHER_ROW_TILE, BF16_SUBLANES)
    return pl.pallas_call(
        _gather_rows_body,
        grid_spec=pltpu.PrefetchScalarGridSpec(
            num_scalar_prefetch=1,
            grid=(p_rows // tg,),
            in_specs=[pl.BlockSpec(memory_space=pl.ANY)],
            out_specs=pl.BlockSpec((tg, d), lambda i, src: (i, 0)),
            scratch_shapes=[pltpu.VMEM((tg * SLAB_ROWS, LANES), F32),
                            pltpu.VMEM((tg * SLAB_ROWS, LANES), F32),
                            pltpu.SemaphoreType.DMA((2,))]),
        out_shape=jax.ShapeDtypeStruct((p_rows, d), BF16),
        compiler_params=_cparams(1),
        name="moe_gather",
    )(row_src, xs)


def _first_tile_of_expert(te_ref, i):
    return jnp.logical_or(i == 0, te_ref[i] != te_ref[jnp.maximum(i - 1, 0)])


def _for_valid_rows(rows, o_ref, compute):
    tm = o_ref.shape[0]
    sub = min(MOE_SUB_ROWS, tm)
    for m in range(sub, tm + 1, sub):
        @pl.when(jnp.logical_and(rows > m - sub, rows <= m))
        def _(m=m):
            o_ref[tm - m:, :] = compute(tm - m)
            if m < tm:
                o_ref[:tm - m, :] = jnp.zeros((tm - m, o_ref.shape[1]), o_ref.dtype)

    @pl.when(rows <= 0)
    def _():
        o_ref[...] = jnp.zeros(o_ref.shape, o_ref.dtype)


def _moe_swiglu_body(te_ref, tr_ref, nl_ref, x_ref, wg_ref, wu_ref, o_ref, wgb_ref, wub_ref):
    i = pl.program_id(1)

    def emit(dot):
        def compute(row0):
            x = x_ref[row0:, :]
            g = dot(x, wg_ref, wgb_ref)
            u = dot(x, wu_ref, wub_ref)
            return _swiglu_act(g, u).astype(o_ref.dtype)

        _for_valid_rows(tr_ref[i], o_ref, compute)

    _with_weights(_first_tile_of_expert(te_ref, i), emit)


def _moe_down_body(te_ref, tr_ref, nl_ref, x_ref, w_ref, o_ref, wb_ref):
    i = pl.program_id(1)

    def emit(dot):
        _for_valid_rows(tr_ref[i], o_ref, lambda row0: dot(x_ref[row0:, :], w_ref, wb_ref))

    _with_weights(_first_tile_of_expert(te_ref, i), emit)


def _live_tile(i, nl):
    return jnp.minimum(i, nl[0] - 1)


def _moe_swiglu(xg, w_gate_up, groups, tm):
    p_rows, k = xg.shape
    f = w_gate_up.shape[2] // 2
    tn = _tile(f, COL_TILE)
    nj = f // tn
    tile_expert, tile_rows, n_live = _tile_table(groups, p_rows, tm)
    return pl.pallas_call(
        _moe_swiglu_body,
        grid_spec=pltpu.PrefetchScalarGridSpec(
            num_scalar_prefetch=3,
            grid=(nj, p_rows // tm),
            in_specs=[pl.BlockSpec((tm, k), lambda j, i, te, tr, nl: (_live_tile(i, nl), 0)),
                      pl.BlockSpec((None, k, tn), lambda j, i, te, tr, nl: (te[i], 0, j)),
                      pl.BlockSpec((None, k, tn), lambda j, i, te, tr, nl: (te[i], 0, nj + j))],
            out_specs=pl.BlockSpec((tm, tn), lambda j, i, te, tr, nl: (i, j)),
            scratch_shapes=[pltpu.VMEM((k, tn), BF16), pltpu.VMEM((k, tn), BF16)]),
        out_shape=jax.ShapeDtypeStruct((p_rows, f), BF16),
        compiler_params=_cparams(2),
        name="moe_swiglu",
    )(tile_expert, tile_rows, n_live, xg, w_gate_up, w_gate_up)


def _moe_down(act, w_down, groups, tm):
    p_rows, k = act.shape
    n = w_down.shape[2]
    tn = _tile(n, COL_TILE)
    tile_expert, tile_rows, n_live = _tile_table(groups, p_rows, tm)
    return pl.pallas_call(
        _moe_down_body,
        grid_spec=pltpu.PrefetchScalarGridSpec(
            num_scalar_prefetch=3,
            grid=(n // tn, p_rows // tm),
            in_specs=[pl.BlockSpec((tm, k), lambda j, i, te, tr, nl: (_live_tile(i, nl), 0)),
                      pl.BlockSpec((None, k, tn), lambda j, i, te, tr, nl: (te[i], 0, j))],
            out_specs=pl.BlockSpec((tm, tn), lambda j, i, te, tr, nl: (i, j)),
            scratch_shapes=[pltpu.VMEM((k, tn), BF16)]),
        out_shape=jax.ShapeDtypeStruct((p_rows, n), F32),
        compiler_params=_cparams(2),
        name="moe_down",
    )(tile_expert, tile_rows, n_live, act, w_down)


def _combine_body(dest_ref, h_ref, w_ref, y_hbm, o_ref, ya0_ref, yb0_ref, ya1_ref, yb1_ref, sems):
    tc = h_ref.shape[0]
    i = pl.program_id(0)
    bufs = ((ya0_ref, yb0_ref), (ya1_ref, yb1_ref))

    def start_tile(tile, slot):
        def start(g, carry):
            for u in range(DMA_ISSUE_UNROLL):
                r = g * DMA_ISSUE_UNROLL + u
                pair = (tile * tc + r) * EXPERT_TOP_K
                for c in range(EXPERT_TOP_K):
                    pltpu.make_async_copy(y_hbm.at[pl.ds(dest_ref[pair + c], 1), :],
                                          bufs[slot][c].at[pl.ds(r, 1), :],
                                          sems.at[slot, c]).start(priority=c % 2)
            return carry

        lax.fori_loop(0, tc // DMA_ISSUE_UNROLL, start, 0)

    def finish_tile(slot):
        for c in range(EXPERT_TOP_K):
            pltpu.make_async_copy(y_hbm.at[pl.ds(0, tc), :], bufs[slot][c], sems.at[slot, c]).wait()
        w = w_ref[...]
        o_ref[...] = h_ref[...] + (w[:, 0:1] * bufs[slot][0][...] + w[:, 1:2] * bufs[slot][1][...])

    @pl.when(i == 0)
    def _():
        start_tile(0, 0)

    for slot in (0, 1):
        @pl.when(jnp.logical_and(i % 2 == slot, i + 1 < pl.num_programs(0)))
        def _(slot=slot):
            start_tile(i + 1, 1 - slot)

        @pl.when(i % 2 == slot)
        def _(slot=slot):
            finish_tile(slot)


def _combine(h, top_w, y, dest):
    n, d = h.shape
    tc = _tile(n, GATHER_ROW_TILE, 8)
    return pl.pallas_call(
        _combine_body,
        grid_spec=pltpu.PrefetchScalarGridSpec(
            num_scalar_prefetch=1,
            grid=(n // tc,),
            in_specs=[pl.BlockSpec((tc, d), lambda i, dst: (i, 0)),
                      pl.BlockSpec((tc, EXPERT_TOP_K), lambda i, dst: (i, 0)),
                      pl.BlockSpec(memory_space=pl.ANY)],
            out_specs=pl.BlockSpec((tc, d), lambda i, dst: (i, 0)),
            scratch_shapes=[pltpu.VMEM((tc, d), F32) for _ in range(2 * EXPERT_TOP_K)]
            + [pltpu.SemaphoreType.DMA((2, EXPERT_TOP_K))]),
        out_shape=jax.ShapeDtypeStruct((n, d), F32),
        compiler_params=_cparams(1),
        name="moe_combine",
    )(dest, h, top_w, y)


def kernel(x, attn_norm, w_qkv, q_norm, k_norm, w_attn_out, ffn_norm, w_ffn_gate_up, w_ffn_down,
           conv_norm, w_conv_in, conv_w, w_conv_out, moe_norm, w_router, w_expert_gate_up,
           w_expert_down):
    batch, seq, d = x.shape
    head_dim = q_norm.shape[-1]
    n_heads = d // head_dim
    n_experts = w_router.shape[-1]
    assert attn_norm.shape[0] == 1 and conv_norm.shape[0] == 1, "two-layer trunk: attention, then conv"
    n_tok = batch * seq
    h = x.reshape(n_tok, d)

    gain = jnp.concatenate([jnp.tile(q_norm[0] * (LOG2E / math.sqrt(head_dim)), n_heads),
                            jnp.tile(k_norm[0], n_heads),
                            jnp.ones((d,), F32)]).reshape(1, 3 * d)
    xn = _rmsnorm(h, attn_norm[0])
    qkv = _mm(xn, w_qkv[0], qk_gain=gain, n_norm_cols=2 * d, head_dim=head_dim)
    o = _moba_attention(qkv, batch, seq, n_heads, head_dim)
    h = _mm(o, w_attn_out[0], residual=h)
    xn = _rmsnorm(h, ffn_norm[0])
    act = _mm_swiglu(xn, w_ffn_gate_up[0])
    h = _mm_long_k_residual(act, w_ffn_down[0], h)

    xn = _rmsnorm(h, conv_norm[0])
    bch = _mm(xn, w_conv_in[0])
    z = _conv_gate(bch, conv_w[0], seq)
    h = _mm(z, w_conv_out[0], residual=h)
    xs, top_idx, top_w = _router(h, moe_norm[0], w_router[0])
    tm = _tile(n_tok, MOE_ROW_TILE, BF16_SUBLANES)
    dest, row_src, groups = _group_by_expert(top_idx, n_experts, tm)
    xg = _gather_rows(xs, row_src, d)
    act = _moe_swiglu(xg, w_expert_gate_up[0], groups, tm)
    y = _moe_down(act, w_expert_down[0], groups, tm)
    h = _combine(h, top_w, y, dest)
    return h.reshape(batch, seq, d)
```
